```python
import math, functools
import jax, jax.numpy as jnp
from jax import lax
import numpy as np

D_MODEL = 4096
BATCH = 4
SEQ = 2048
DEPTH = 2
DEC_BATCH = 128
DEC_SEQ = 8
PAST_LEN = 16384
PAGE_SIZE = 128

BRANCH_W = D_MODEL // 2
N_BRANCH = 3
V_HD = 128
MLA_HEADS = BRANCH_W // V_HD
NOPE_HD = 128
ROPE_HD = 64
QK_HD = NOPE_HD + ROPE_HD
Q_LORA = D_MODEL // 4
KV_LORA = 512
ROPE_THETA = 10000.0
Q_BLOCK = 128
GLA_HEADS = 4
GLA_KD = D_MODEL // 4
GLA_VD = BRANCH_W
GLA_KHD = GLA_KD // GLA_HEADS
GLA_VHD = GLA_VD // GLA_HEADS
GLA_GATE_RANK = 16
GLA_GATE_TAU = 16.0
GLA_CHUNK = 64
N_MEM = 256
MEM_HEADS = 4
MEM_HD = BRANCH_W // MEM_HEADS
EPS = 1e-6

IN_SIZES = (Q_LORA, KV_LORA, ROPE_HD, BRANCH_W,
            GLA_KD, GLA_KD, GLA_VD, GLA_GATE_RANK, BRANCH_W,
            BRANCH_W, BRANCH_W,
            N_BRANCH * D_MODEL)
IN_COLS = sum(IN_SIZES)

kernel_name = "hybrid_mla_gla_memory_step"


def _split_points():
    return [int(v) for v in np.cumsum(IN_SIZES)[:-1]]


def rms_norm(x, g):
    xf = x.astype(jnp.float32)
    y = xf * lax.rsqrt(jnp.mean(xf * xf, axis=-1, keepdims=True) + EPS)
    return (y * g.astype(jnp.float32)).astype(x.dtype)


def rope(x, pos):
    half = ROPE_HD // 2
    inv = jnp.power(ROPE_THETA, -jnp.arange(half, dtype=jnp.float32) / half)
    ang = pos.astype(jnp.float32)[:, None] * inv[None, :]
    cos, sin = jnp.cos(ang)[:, None, :], jnp.sin(ang)[:, None, :]
    x1, x2 = x[..., :half], x[..., half:]
    return jnp.concatenate([x1 * cos - x2 * sin, x1 * sin + x2 * cos], axis=-1).astype(x.dtype)


def mla_queries(cq, pos, w_uq, q_lora_g, qk_g):
    q = rms_norm(cq, q_lora_g) @ w_uq
    q = rms_norm(q.reshape(q.shape[:-1] + (MLA_HEADS, QK_HD)), qk_g)
    return jnp.concatenate([q[..., :NOPE_HD], rope(q[..., NOPE_HD:], pos)], axis=-1)


def mla_keys(ckv_n, kpe, pos, w_uk, k_g):
    k_nope = (ckv_n @ w_uk).reshape(ckv_n.shape[:-1] + (MLA_HEADS, NOPE_HD))
    k_pe = jnp.broadcast_to(kpe[..., None, :], k_nope.shape[:-1] + (ROPE_HD,))
    k = rms_norm(jnp.concatenate([k_nope, k_pe], axis=-1), k_g)
    return jnp.concatenate([k[..., :NOPE_HD], rope(k[..., NOPE_HD:], pos)], axis=-1)


def mla_attend(q, k, ckv_n, q_pos, k_pos, w_uv):
    s = jnp.einsum('...qhd,...khd->...hqk', q, k, preferred_element_type=jnp.float32) * QK_HD ** -0.5
    s = jnp.where(k_pos[None, :] <= q_pos[:, None], s, -jnp.inf)
    p = jax.nn.softmax(s, axis=-1).astype(ckv_n.dtype)
    o_lat = jnp.einsum('...hqk,...kc->...qhc', p, ckv_n)
    o = jnp.einsum('...qhc,chv->...qhv', o_lat, w_uv.reshape(KV_LORA, MLA_HEADS, V_HD))
    return o.reshape(o.shape[:-2] + (BRANCH_W,))


def mla_prompt_attend(q, ckv_n, kpe, pos, w_uk, k_g, w_uv):
    k = mla_keys(ckv_n, kpe, pos, w_uk, k_g)
    b, s = q.shape[:2]
    nb = s // Q_BLOCK
    qb = q.reshape(b, nb, Q_BLOCK, MLA_HEADS, QK_HD).swapaxes(0, 1)
    pb = pos.reshape(nb, Q_BLOCK)
    o = lax.map(lambda a: mla_attend(a[0], k, ckv_n, a[1], pos, w_uv), (qb, pb))
    return o.swapaxes(0, 1).reshape(b, s, BRANCH_W)


def mla_sample_attend(q, ckv_n, kpe, pos, ckv_pool, kpe_pool, page_table, layer, w_uk, k_g, w_uv):
    past = page_table.shape[1] * PAGE_SIZE
    k_pos = jnp.concatenate([jnp.arange(past, dtype=jnp.int32), pos])

    def one(args):
        pt, q_i, c_i, p_i = args
        c_all = jnp.concatenate([ckv_pool[layer, pt].reshape(past, KV_LORA), c_i], axis=0)
        p_all = jnp.concatenate([kpe_pool[layer, pt].reshape(past, ROPE_HD), p_i], axis=0)
        k_all = mla_keys(c_all, p_all, k_pos, w_uk, k_g)
        return mla_attend(q_i, k_all, c_all, pos, k_pos, w_uv)

    return lax.map(one, (page_table, q, ckv_n, kpe))


def gla_recurrence(q, k, v, logf, s0):
    b, L = q.shape[:2]
    c = math.gcd(L, GLA_CHUNK)
    n = L // c

    def chunks(a):
        return a.reshape(b, n, c, GLA_HEADS, a.shape[-1]).transpose(1, 0, 3, 2, 4).astype(jnp.float32)

    causal = jnp.tril(jnp.ones((c, c), dtype=bool))

    def step(S, xs):
        qi, ki, vi, gi = xs
        cum = jnp.cumsum(gi, axis=2)
        inter = jnp.einsum('bhtd,bhdv->bhtv', qi * jnp.exp(cum), S)
        diff = cum[:, :, :, None, :] - cum[:, :, None, :, :]
        diff = jnp.where(causal[None, None, :, :, None], diff, -jnp.inf)
        a = jnp.einsum('bhtd,bhsd,bhtsd->bhts', qi, ki, jnp.exp(diff))
        intra = jnp.einsum('bhts,bhsv->bhtv', a, vi)
        last = cum[:, :, -1:, :]
        S_new = jnp.exp(last[:, :, 0, :])[..., None] * S + jnp.einsum('bhsd,bhsv->bhdv', ki * jnp.exp(last - cum), vi)
        return S_new, inter + intra

    S, o = lax.scan(step, s0.astype(jnp.float32), (chunks(q), chunks(k), chunks(v), chunks(logf)))
    o = o.transpose(1, 0, 3, 2, 4).reshape(b, L, GLA_HEADS, v.shape[-1])
    return o, S.astype(v.dtype)


def memory_kv(mem, mem_norm_w, w_mk, w_mv, mem_k_g):
    m = rms_norm(mem, mem_norm_w)
    b = mem.shape[0]
    k = rms_norm((m @ w_mk).reshape(b, N_MEM, MEM_HEADS, MEM_HD), mem_k_g)
    v = (m @ w_mv).reshape(b, N_MEM, MEM_HEADS, MEM_HD)
    return k, v


def memory_attend(q, k, v):
    s = jnp.einsum('blhd,bmhd->bhlm', q, k, preferred_element_type=jnp.float32) * MEM_HD ** -0.5
    p = jax.nn.softmax(s, axis=-1).astype(v.dtype)
    o = jnp.einsum('bhlm,bmhd->blhd', p, v)
    return o.reshape(q.shape[0], q.shape[1], BRANCH_W)


def mixer_layer(x, pos, mla_attn, mem_k, mem_v, gla_s0, norm_w, w_in, q_lora_g, kv_lora_g, w_uq,
                mla_q_g, w_gate2, b_gate, gla_o_g, mem_q_g, w_branch, b_merge, w_out):
    b, L = x.shape[:2]
    h = rms_norm(x, norm_w)
    z = h @ w_in
    (cq, ckv, kpe, g_a, gq, gk, gv, g_lr, g_b, mq, g_c, m_logit) = jnp.split(z, _split_points(), axis=-1)
    q_a = mla_queries(cq, pos, w_uq, q_lora_g, mla_q_g)
    ckv_n = rms_norm(ckv, kv_lora_g)
    o_a = mla_attn(q_a, ckv_n, kpe)
    q_b = gq.reshape(b, L, GLA_HEADS, GLA_KHD) * GLA_KHD ** -0.5
    k_b = gk.reshape(b, L, GLA_HEADS, GLA_KHD)
    v_b = gv.reshape(b, L, GLA_HEADS, GLA_VHD)
    logf = jax.nn.log_sigmoid((g_lr @ w_gate2 + b_gate).astype(jnp.float32)) / GLA_GATE_TAU
    o_b, s_b = gla_recurrence(q_b, k_b, v_b, logf.reshape(b, L, GLA_HEADS, GLA_KHD), gla_s0)
    o_b = rms_norm(o_b, gla_o_g).astype(x.dtype).reshape(b, L, BRANCH_W)
    q_c = rms_norm(mq.reshape(b, L, MEM_HEADS, MEM_HD), mem_q_g)
    o_c = memory_attend(q_c, mem_k, mem_v)
    br = jnp.stack([o_a * jax.nn.silu(g_a), o_b * jax.nn.silu(g_b), o_c * jax.nn.silu(g_c)], axis=2)
    proj = jnp.einsum('blnw,nwd->blnd', br, w_branch)
    gates = jax.nn.sigmoid(m_logit.reshape(b, L, N_BRANCH, D_MODEL) + b_merge)
    y = x + jnp.sum(gates * proj, axis=2) @ w_out
    return y, ckv_n, kpe, s_b


def setup_inputs(seed: int = 0) -> dict:
    key = jax.random.key(seed)
    ks = iter(jax.random.split(key, 40))

    def nrm(shape, scale=1.0):
        return jax.random.normal(next(ks), shape, jnp.float32) * scale

    def nrm_layers(shape):
        return jax.vmap(lambda k: jax.random.normal(k, shape, jnp.float32))(jax.random.split(next(ks), DEPTH))

    def gain(shape):
        return 1.0 + 0.01 * jax.random.normal(next(ks), shape, jnp.float32)

    n_pages = PAST_LEN // PAGE_SIZE
    n_used = DEC_BATCH * n_pages
    n_phys = (5 * n_used) // 4
    page_table = jax.random.permutation(next(ks), n_phys)[:n_used].reshape(DEC_BATCH, n_pages).astype(jnp.int32)
    return {
        "x_prompt": nrm((BATCH, SEQ, D_MODEL)),
        "x_sample": nrm((DEC_BATCH, DEC_SEQ, D_MODEL)),
        "mem_prompt": nrm((BATCH, N_MEM, D_MODEL)),
        "cache_ckv": nrm_layers((n_phys, PAGE_SIZE, KV_LORA)),
        "cache_kpe": nrm_layers((n_phys, PAGE_SIZE, ROPE_HD)),
        "cache_mem_k": nrm((DEPTH, DEC_BATCH, N_MEM, MEM_HEADS, MEM_HD)),
        "cache_mem_v": nrm((DEPTH, DEC_BATCH, N_MEM, MEM_HEADS, MEM_HD)),
        "state_gla": nrm((DEPTH, DEC_BATCH, GLA_HEADS, GLA_KHD, GLA_VHD)),
        "page_table": page_table,
        "norm_w": gain((DEPTH, D_MODEL)),
        "w_in": nrm((DEPTH, D_MODEL, IN_COLS), D_MODEL ** -0.5),
        "q_lora_g": gain((DEPTH, Q_LORA)),
        "kv_lora_g": gain((DEPTH, KV_LORA)),
        "w_uq": nrm((DEPTH, Q_LORA, MLA_HEADS * QK_HD), Q_LORA ** -0.5),
        "mla_q_g": gain((DEPTH, QK_HD)),
        "mla_k_g": gain((DEPTH, QK_HD)),
        "w_uk": nrm((DEPTH, KV_LORA, MLA_HEADS * NOPE_HD), KV_LORA ** -0.5),
        "w_uv": nrm((DEPTH, KV_LORA, MLA_HEADS * V_HD), KV_LORA ** -0.5),
        "w_gate2": nrm((DEPTH, GLA_GATE_RANK, GLA_KD), GLA_GATE_RANK ** -0.5),
        "b_gate": nrm((DEPTH, GLA_KD), 0.01),
        "gla_o_g": gain((DEPTH, GLA_VHD)),
        "mem_norm_w": gain((DEPTH, D_MODEL)),
        "w_mk": nrm((DEPTH, D_MODEL, BRANCH_W), D_MODEL ** -0.5),
        "w_mv": nrm((DEPTH, D_MODEL, BRANCH_W), D_MODEL ** -0.5),
        "mem_q_g": gain((DEPTH, MEM_HD)),
        "mem_k_g": gain((DEPTH, MEM_HD)),
        "w_branch": nrm((DEPTH, N_BRANCH, BRANCH_W, D_MODEL), BRANCH_W ** -0.5),
        "b_merge": nrm((DEPTH, N_BRANCH, D_MODEL), 0.01),
        "w_out": nrm((DEPTH, D_MODEL, D_MODEL), D_MODEL ** -0.5),
    }


def reference(x_prompt, x_sample, mem_prompt, cache_ckv, cache_kpe, cache_mem_k, cache_mem_v, state_gla,
              page_table, norm_w, w_in, q_lora_g, kv_lora_g, w_uq, mla_q_g, mla_k_g, w_uk, w_uv, w_gate2,
              b_gate, gla_o_g, mem_norm_w, w_mk, w_mv, mem_q_g, mem_k_g, w_branch, b_merge, w_out):
    pos_p = jnp.arange(x_prompt.shape[1], dtype=jnp.int32)
    past = page_table.shape[1] * PAGE_SIZE
    pos_s = past + jnp.arange(x_sample.shape[1], dtype=jnp.int32)
    gla0_p = jnp.zeros((x_prompt.shape[0], GLA_HEADS, GLA_KHD, GLA_VHD), x_prompt.dtype)
    xp, xs = x_prompt, x_sample
    ckv_p_l, kpe_p_l, gla_p_l, mk_p_l, mv_p_l = [], [], [], [], []
    ckv_s_l, kpe_s_l, gla_s_l = [], [], []
    for l in range(DEPTH):
        shared = (norm_w[l], w_in[l], q_lora_g[l], kv_lora_g[l], w_uq[l], mla_q_g[l], w_gate2[l], b_gate[l],
                  gla_o_g[l], mem_q_g[l], w_branch[l], b_merge[l], w_out[l])
        mk, mv = memory_kv(mem_prompt, mem_norm_w[l], w_mk[l], w_mv[l], mem_k_g[l])
        attn_p = functools.partial(mla_prompt_attend, pos=pos_p, w_uk=w_uk[l], k_g=mla_k_g[l], w_uv=w_uv[l])
        xp, ckv_p, kpe_p, gla_p = mixer_layer(xp, pos_p, attn_p, mk, mv, gla0_p, *shared)
        attn_s = functools.partial(mla_sample_attend, pos=pos_s, ckv_pool=cache_ckv, kpe_pool=cache_kpe,
                                   page_table=page_table, layer=l, w_uk=w_uk[l], k_g=mla_k_g[l], w_uv=w_uv[l])
        xs, ckv_s, kpe_s, gla_s = mixer_layer(xs, pos_s, attn_s, cache_mem_k[l], cache_mem_v[l], state_gla[l], *shared)
        ckv_p_l.append(ckv_p); kpe_p_l.append(kpe_p); gla_p_l.append(gla_p); mk_p_l.append(mk); mv_p_l.append(mv)
        ckv_s_l.append(ckv_s); kpe_s_l.append(kpe_s); gla_s_l.append(gla_s)
    new_ckv_p = jnp.stack(ckv_p_l)
    new_kpe_p = jnp.stack(kpe_p_l)
    new_gla_p = jnp.stack(gla_p_l)
    new_mk_p = jnp.stack(mk_p_l)
    new_mv_p = jnp.stack(mv_p_l)
    new_ckv_s = jnp.stack(ckv_s_l)
    new_kpe_s = jnp.stack(kpe_s_l)
    new_gla_s = jnp.stack(gla_s_l)
    return (xp, xs, new_ckv_p, new_kpe_p, new_gla_p, new_mk_p, new_mv_p, new_ckv_s, new_kpe_s, new_gla_s)
```

```python
import functools
import math
from types import SimpleNamespace

import jax
import jax.numpy as jnp
import numpy as np
from jax import lax
from jax.experimental import pallas as pl
from jax.experimental.pallas import tpu as pltpu

F32 = jnp.float32
BF16 = jnp.bfloat16
EPS = 1e-6
LANES = 128
SUBLANES = 8
VMEM_LIMIT = 56 * 1024 * 1024

NT_DIMS = (((1,), (1,)), ((), ()))
TN_DIMS = (((0,), (0,)), ((), ()))


def default_config():
    d_model = 4096
    branch_w = d_model // 2
    c = SimpleNamespace(
        d_model=d_model, branch_w=branch_w, n_branch=3,
        v_hd=128, nope_hd=128, rope_hd=64, q_lora=d_model // 4, kv_lora=512,
        rope_theta=10000.0,
        gla_heads=4, gla_kd=d_model // 4, gla_vd=branch_w, gate_rank=16, gate_tau=16.0,
        gla_chunk=64,
        mem_heads=4, page_size=128,
        tm_mm=1024, tn_mm=1024, tm_rows=512, tm_prep=256, tq_flash=512, tq_mem=512,
        pages_per_step=16, tm_merge=512, tn_merge=512,
    )
    return c


def _derive(c):
    c.mla_heads = c.branch_w // c.v_hd
    c.qk_hd = c.nope_hd + c.rope_hd
    c.hp = c.nope_hd + LANES
    c.gla_khd = c.gla_kd // c.gla_heads
    c.gla_vhd = c.gla_vd // c.gla_heads
    c.mem_hd = c.branch_w // c.mem_heads
    c.in_sizes = (c.q_lora, c.kv_lora, c.rope_hd, c.branch_w, c.gla_kd, c.gla_kd, c.gla_vd,
                  c.gate_rank, c.branch_w, c.branch_w, c.branch_w, c.n_branch * c.d_model)
    assert c.rope_hd + c.gate_rank <= LANES and c.nope_hd % LANES == 0
    return c


def _params(sem):
    return pltpu.CompilerParams(dimension_semantics=sem, vmem_limit_bytes=VMEM_LIMIT)


def _silu(g):
    return g / (1.0 + jnp.exp(-g))


def _sigmoid(x):
    return 1.0 / (1.0 + jnp.exp(-x))


def _rms_rows_kernel(x_ref, g_ref, o_ref):
    x = x_ref[...]
    ms = jnp.mean(x * x, axis=-1, keepdims=True)
    o_ref[...] = (x * lax.rsqrt(ms + EPS) * g_ref[...]).astype(o_ref.dtype)


def rms_rows(x, g, tm):
    t, d = x.shape
    tm = min(tm, t)
    return pl.pallas_call(
        _rms_rows_kernel, grid=(t // tm,),
        in_specs=[pl.BlockSpec((tm, d), lambda i: (i, 0)), pl.BlockSpec((1, d), lambda i: (0, 0))],
        out_specs=pl.BlockSpec((tm, d), lambda i: (i, 0)),
        out_shape=jax.ShapeDtypeStruct((t, d), BF16),
        compiler_params=_params(("parallel",)), name="rms_rows",
    )(x, g.reshape(1, d))


def _mm_kernel(a_ref, b_ref, *rest, mode):
    acc = jnp.dot(a_ref[...].astype(BF16), b_ref[...], preferred_element_type=F32)
    if mode == "plain":
        (o_ref,) = rest
    elif mode == "tile_norm":
        g_ref, o_ref = rest
        ms = jnp.mean(acc * acc, axis=-1, keepdims=True)
        acc = acc * lax.rsqrt(ms + EPS) * g_ref[...]
    elif mode == "residual":
        x_ref, o_ref = rest
        acc = x_ref[...] + acc
    o_ref[...] = acc.astype(o_ref.dtype)


def matmul(a, b, *, tm, tn, out_dtype=F32, mode="plain", extra=None, name="matmul"):
    m, k = a.shape
    n = b.shape[1]
    tm, tn = min(tm, m), min(tn, n)
    assert m % tm == 0 and n % tn == 0, (m, n, tm, tn)
    in_specs = [pl.BlockSpec((tm, k), lambda i, j: (i, 0)), pl.BlockSpec((k, tn), lambda i, j: (0, j))]
    args = [a, b]
    if mode == "tile_norm":
        in_specs.append(pl.BlockSpec((1, tn), lambda i, j: (0, 0)))
        args.append(extra.reshape(1, tn))
    elif mode == "residual":
        in_specs.append(pl.BlockSpec((tm, tn), lambda i, j: (i, j)))
        args.append(extra)
    return pl.pallas_call(
        functools.partial(_mm_kernel, mode=mode), grid=(m // tm, n // tn),
        in_specs=in_specs, out_specs=pl.BlockSpec((tm, tn), lambda i, j: (i, j)),
        out_shape=jax.ShapeDtypeStruct((m, n), out_dtype),
        compiler_params=_params(("parallel", "parallel")), name=name,
    )(*args)


def _rope_slab(x, cos, sin_lo, sin_hi, half):
    return x * cos + pltpu.roll(x, LANES - half, 1) * sin_lo + pltpu.roll(x, half, 1) * sin_hi


def _q_prep_kernel(cq_ref, gl_ref, w_ref, gq_ref, cos_ref, slo_ref, shi_ref, o_ref, *, c):
    cq = cq_ref[...]
    ms = jnp.mean(cq * cq, axis=-1, keepdims=True)
    cqn = (cq * lax.rsqrt(ms + EPS) * gl_ref[...]).astype(BF16)
    q = jnp.dot(cqn, w_ref[...], preferred_element_type=F32)
    cos, slo, shi = cos_ref[...], slo_ref[...], shi_ref[...]
    g_n, g_p = gq_ref[:, :c.nope_hd], gq_ref[:, c.nope_hd:]
    for h in range(c.mla_heads):
        lo = h * c.hp
        qn = q[:, lo:lo + c.nope_hd]
        qp = q[:, lo + c.nope_hd:lo + c.hp]
        ss = jnp.sum(qn * qn, axis=-1, keepdims=True) + jnp.sum(qp * qp, axis=-1, keepdims=True)
        r = lax.rsqrt(ss / c.qk_hd + EPS)
        o_ref[:, lo:lo + c.nope_hd] = (qn * r * g_n).astype(o_ref.dtype)
        roped = _rope_slab(qp * r * g_p, cos, slo, shi, c.rope_hd // 2)
        o_ref[:, lo + c.nope_hd:lo + c.hp] = roped.astype(o_ref.dtype)


def q_prep(c, z1, q_lora_g, w_uq_pad, qg_pad, tabs):
    t = z1.shape[0]
    tm = min(c.tm_prep, t)
    n = c.mla_heads * c.hp
    row = lambda i: (i, 0)
    const = lambda i: (0, 0)
    return pl.pallas_call(
        functools.partial(_q_prep_kernel, c=c), grid=(t // tm,),
        in_specs=[pl.BlockSpec((tm, c.q_lora), row), pl.BlockSpec((1, c.q_lora), const),
                  pl.BlockSpec((c.q_lora, n), const), pl.BlockSpec((1, c.hp), const),
                  pl.BlockSpec((tm, LANES), row), pl.BlockSpec((tm, LANES), row),
                  pl.BlockSpec((tm, LANES), row)],
        out_specs=pl.BlockSpec((tm, n), row),
        out_shape=jax.ShapeDtypeStruct((t, n), BF16),
        compiler_params=_params(("parallel",)), name="q_prep",
    )(z1, q_lora_g.reshape(1, -1), w_uq_pad, qg_pad, *tabs)


def _kv_prep_kernel(ckv_ref, slab_ref, gl_ref, wuk_ref, wuv_ref, kg_ref, cos_ref, slo_ref, shi_ref,
                    ckvn_ref, kpe_ref, k_ref, v_ref, *, c):
    ckv = ckv_ref[...]
    ms = jnp.mean(ckv * ckv, axis=-1, keepdims=True)
    cn = ckv * lax.rsqrt(ms + EPS) * gl_ref[...]
    ckvn_ref[...] = cn
    slab = slab_ref[...]
    kpe_ref[...] = slab[:, :c.rope_hd]
    lane = lax.broadcasted_iota(jnp.int32, slab.shape, 1)
    kp = jnp.where(lane < c.rope_hd, slab, 0.0)
    cb = cn.astype(BF16)
    kn = jnp.dot(cb, wuk_ref[...], preferred_element_type=F32)
    v_ref[...] = jnp.dot(cb, wuv_ref[...], preferred_element_type=F32).astype(v_ref.dtype)
    sspe = jnp.sum(kp * kp, axis=-1, keepdims=True)
    g_n, g_p = kg_ref[:, :c.nope_hd], kg_ref[:, c.nope_hd:]
    rp = _rope_slab(kp * g_p, cos_ref[...], slo_ref[...], shi_ref[...], c.rope_hd // 2)
    for h in range(c.mla_heads):
        knh = kn[:, h * c.nope_hd:(h + 1) * c.nope_hd]
        ss = jnp.sum(knh * knh, axis=-1, keepdims=True) + sspe
        r = lax.rsqrt(ss / c.qk_hd + EPS)
        lo = h * c.hp
        k_ref[:, lo:lo + c.nope_hd] = (knh * r * g_n).astype(k_ref.dtype)
        k_ref[:, lo + c.nope_hd:lo + c.hp] = (rp * r).astype(k_ref.dtype)


def kv_prep(c, z1, kv_lora_g, w_uk, w_uv, kg_pad, tabs):
    t = z1.shape[0]
    tm = min(c.tm_prep, t)
    row = lambda i: (i, 0)
    const = lambda i: (0, 0)
    ckv_blk = c.q_lora // c.kv_lora
    slab_blk = (c.q_lora + c.kv_lora) // LANES
    assert c.q_lora % c.kv_lora == 0
    nk, nv = c.mla_heads * c.hp, c.mla_heads * c.v_hd
    return pl.pallas_call(
        functools.partial(_kv_prep_kernel, c=c), grid=(t // tm,),
        in_specs=[pl.BlockSpec((tm, c.kv_lora), lambda i: (i, ckv_blk)),
                  pl.BlockSpec((tm, LANES), lambda i: (i, slab_blk)),
                  pl.BlockSpec((1, c.kv_lora), const),
                  pl.BlockSpec(w_uk.shape, const), pl.BlockSpec(w_uv.shape, const),
                  pl.BlockSpec((1, c.hp), const),
                  pl.BlockSpec((tm, LANES), row), pl.BlockSpec((tm, LANES), row),
                  pl.BlockSpec((tm, LANES), row)],
        out_specs=[pl.BlockSpec((tm, c.kv_lora), row), pl.BlockSpec((tm, c.rope_hd), row),
                   pl.BlockSpec((tm, nk), row), pl.BlockSpec((tm, nv), row)],
        out_shape=[jax.ShapeDtypeStruct((t, c.kv_lora), F32), jax.ShapeDtypeStruct((t, c.rope_hd), F32),
                   jax.ShapeDtypeStruct((t, nk), BF16), jax.ShapeDtypeStruct((t, nv), BF16)],
        compiler_params=_params(("parallel",)), name="kv_prep",
    )(z1, z1, kv_lora_g.reshape(1, -1), w_uk, w_uv, kg_pad, *tabs)


def _flash_kernel(q_ref, k_ref, v_ref, g_ref, o_ref, m_scr, l_scr, acc_scr, *, tq, scale):
    qi, ki = pl.program_id(2), pl.program_id(3)

    @pl.when(ki == 0)
    def _():
        m_scr[...] = jnp.full(m_scr.shape, -jnp.inf, F32)
        l_scr[...] = jnp.zeros(l_scr.shape, F32)
        acc_scr[...] = jnp.zeros(acc_scr.shape, F32)

    @pl.when(ki <= qi)
    def _():
        s = lax.dot_general(q_ref[...], k_ref[...], NT_DIMS, preferred_element_type=F32) * scale
        row = lax.broadcasted_iota(jnp.int32, s.shape, 0) + qi * tq
        col = lax.broadcasted_iota(jnp.int32, s.shape, 1) + ki * tq
        s = jnp.where(col <= row, s, -jnp.inf)
        m_prev = m_scr[...]
        m_new = jnp.maximum(m_prev, jnp.max(s, axis=-1, keepdims=True))
        alpha = jnp.exp(m_prev - m_new)
        p = jnp.exp(s - m_new)
        l_scr[...] = alpha * l_scr[...] + jnp.sum(p, axis=-1, keepdims=True)
        acc_scr[...] = alpha * acc_scr[...] + jnp.dot(p.astype(BF16), v_ref[...], preferred_element_type=F32)
        m_scr[...] = m_new

    @pl.when(ki == pl.num_programs(3) - 1)
    def _():
        o_ref[...] = (acc_scr[...] / l_scr[...] * _silu(g_ref[...])).astype(o_ref.dtype)


def flash_prompt(c, q_full, k_full, v_full, z2, batch, seq):
    tq = min(c.tq_flash, seq)
    nq = seq // tq
    grid = (batch, c.mla_heads, nq, nq)
    qmap = lambda b, h, qi, ki: (b * nq + qi, h)
    kmap = lambda b, h, qi, ki: (b * nq + jnp.minimum(ki, qi), h)
    return pl.pallas_call(
        functools.partial(_flash_kernel, tq=tq, scale=c.qk_hd ** -0.5), grid=grid,
        in_specs=[pl.BlockSpec((tq, c.hp), qmap), pl.BlockSpec((tq, c.hp), kmap),
                  pl.BlockSpec((tq, c.v_hd), kmap), pl.BlockSpec((tq, c.v_hd), qmap)],
        out_specs=pl.BlockSpec((tq, c.v_hd), qmap),
        out_shape=jax.ShapeDtypeStruct((batch * seq, c.branch_w), BF16),
        scratch_shapes=[pltpu.VMEM((tq, 1), F32), pltpu.VMEM((tq, 1), F32), pltpu.VMEM((tq, c.v_hd), F32)],
        compiler_params=_params(("parallel", "parallel", "parallel", "arbitrary")), name="flash_prompt",
    )(q_full, k_full, v_full, z2)


def _score_block(c, nq, lhs_ref, cb, kp, ctab, stab, qpa, qpb):
    hn = c.mla_heads * c.nope_hd
    res = lax.dot_general(lhs_ref[...], cb, NT_DIMS, preferred_element_type=F32)
    sq = kp * kp
    hi = sq.astype(BF16)
    lo = (sq - hi.astype(F32)).astype(BF16)
    ones = jnp.ones((SUBLANES, c.rope_hd), BF16)
    sspe = (lax.dot_general(ones, hi, NT_DIMS, preferred_element_type=F32)
            + lax.dot_general(ones, lo, NT_DIMS, preferred_element_type=F32))[0:1, :]
    s_pe = (lax.dot_general(qpa, (kp * ctab).astype(BF16), NT_DIMS, preferred_element_type=F32)
            + lax.dot_general(qpb, (kp * stab).astype(BF16), NT_DIMS, preferred_element_type=F32))
    out = []
    for h in range(c.mla_heads):
        kh = res[h * c.nope_hd:(h + 1) * c.nope_hd, :]
        ss = jnp.sum(kh * kh, axis=0, keepdims=True) + sspe
        r = lax.rsqrt(ss / c.qk_hd + EPS)
        out.append((res[hn + h * nq:hn + (h + 1) * nq, :] + s_pe[h * nq:(h + 1) * nq, :]) * r)
    return out


def _paged_kernel(pt_ref, wukt_ref, qlat_ref, qpa_ref, qpb_ref, cnew_ref, kpnew_ref, ctab_ref, stab_ref,
                  ctabn_ref, stabn_ref, *rest, c, npages, nq):
    del pt_ref
    c_refs, kp_refs = rest[:npages], rest[npages:2 * npages]
    o_ref, lhs_scr, cbf_scr, s_scr, m_scr, l_scr, acc_scr = rest[2 * npages:]
    b, j = pl.program_id(0), pl.program_id(1)
    hn, hq = c.mla_heads * c.nope_hd, c.mla_heads * nq
    page = c.page_size

    @pl.when((b == 0) & (j == 0))
    def _():
        lhs_scr[0:hn, :] = wukt_ref[...]

    @pl.when(j == 0)
    def _():
        lhs_scr[hn:hn + hq, :] = qlat_ref[...]
        m_scr[...] = jnp.full(m_scr.shape, -jnp.inf, F32)
        l_scr[...] = jnp.zeros(l_scr.shape, F32)
        acc_scr[...] = jnp.zeros(acc_scr.shape, F32)

    qpa, qpb = qpa_ref[...], qpb_ref[...]

    def softmax_update(s, vals):
        m_prev = m_scr[...]
        m_new = jnp.maximum(m_prev, jnp.max(s, axis=-1, keepdims=True))
        alpha = jnp.exp(m_prev - m_new)
        p = jnp.exp(s - m_new)
        l_scr[...] = alpha * l_scr[...] + jnp.sum(p, axis=-1, keepdims=True)
        acc_scr[...] = alpha * acc_scr[...] + jnp.dot(p.astype(BF16), vals, preferred_element_type=F32)
        m_scr[...] = m_new

    for pp in range(npages // 2):
        rows = slice(pp * 2 * page, (pp + 1) * 2 * page)
        cb = jnp.concatenate([c_refs[2 * pp][...], c_refs[2 * pp + 1][...]], axis=0).astype(BF16)
        cbf_scr[rows, :] = cb
        kp = jnp.concatenate([kp_refs[2 * pp][...], kp_refs[2 * pp + 1][...]], axis=0)
        sc = _score_block(c, nq, lhs_scr, cb, kp, ctab_ref[rows, :], stab_ref[rows, :], qpa, qpb)
        for h in range(c.mla_heads):
            s_scr[h * nq:(h + 1) * nq, rows] = sc[h]
    softmax_update(s_scr[...], cbf_scr[...])

    @pl.when(j == pl.num_programs(1) - 1)
    def _():
        pad = page - nq
        cb = jnp.concatenate([cnew_ref[...], jnp.zeros((pad, c.kv_lora), F32)], axis=0).astype(BF16)
        kp = jnp.concatenate([kpnew_ref[...], jnp.zeros((pad, c.rope_hd), F32)], axis=0)
        sc = _score_block(c, nq, lhs_scr, cb, kp, ctabn_ref[...], stabn_ref[...], qpa, qpb)
        s = jnp.concatenate(sc, axis=0)
        col = lax.broadcasted_iota(jnp.int32, s.shape, 1)
        qidx = lax.broadcasted_iota(jnp.int32, s.shape, 0) % nq
        softmax_update(jnp.where(col <= qidx, s, -jnp.inf), cb)
        o_ref[...] = acc_scr[...] / l_scr[...]


def paged_sample_attend(c, layer, page_table, w_ukt, qlat, qpa, qpb, c_new, kp_new, ctab, stab,
                        cache_ckv, cache_kpe):
    nb, n_pages = page_table.shape
    nq = c_new.shape[1]
    hq = c.mla_heads * nq
    hn = c.mla_heads * c.nope_hd
    npg = min(c.pages_per_step, n_pages)
    assert n_pages % npg == 0 and npg % 2 == 0
    steps = n_pages // npg
    rows = npg * c.page_size
    past = n_pages * c.page_size
    const2 = lambda b, j, pt: (0, 0)
    per_b = lambda b, j, pt: (b, 0, 0)
    tab_map = lambda b, j, pt: (j, 0)
    in_specs = [pl.BlockSpec((hn, c.kv_lora), const2),
                pl.BlockSpec((None, hq, c.kv_lora), per_b),
                pl.BlockSpec((None, hq, c.rope_hd), per_b), pl.BlockSpec((None, hq, c.rope_hd), per_b),
                pl.BlockSpec((None, nq, c.kv_lora), per_b), pl.BlockSpec((None, nq, c.rope_hd), per_b),
                pl.BlockSpec((rows, c.rope_hd), tab_map), pl.BlockSpec((rows, c.rope_hd), tab_map),
                pl.BlockSpec((c.page_size, c.rope_hd), const2), pl.BlockSpec((c.page_size, c.rope_hd), const2)]
    for i in range(npg):
        in_specs.append(pl.BlockSpec((None, None, c.page_size, c.kv_lora),
                                     lambda b, j, pt, i=i: (layer, pt[b, j * npg + i], 0, 0)))
    for i in range(npg):
        in_specs.append(pl.BlockSpec((None, None, c.page_size, c.rope_hd),
                                     lambda b, j, pt, i=i: (layer, pt[b, j * npg + i], 0, 0)))
    grid_spec = pltpu.PrefetchScalarGridSpec(
        num_scalar_prefetch=1, grid=(nb, steps), in_specs=in_specs,
        out_specs=pl.BlockSpec((None, hq, c.kv_lora), per_b),
        scratch_shapes=[pltpu.VMEM((hn + hq, c.kv_lora), BF16), pltpu.VMEM((rows, c.kv_lora), BF16),
                        pltpu.VMEM((hq, rows), F32), pltpu.VMEM((hq, 1), F32), pltpu.VMEM((hq, 1), F32),
                        pltpu.VMEM((hq, c.kv_lora), F32)])
    return pl.pallas_call(
        functools.partial(_paged_kernel, c=c, npages=npg, nq=nq), grid_spec=grid_spec,
        out_shape=jax.ShapeDtypeStruct((nb, hq, c.kv_lora), F32),
        compiler_params=_params(("arbitrary", "arbitrary")), name="paged_sample_attend",
    )(page_table, w_ukt, qlat, qpa, qpb, c_new, kp_new, ctab[:past], stab[:past], ctab[past:], stab[past:],
      *([cache_ckv] * npg), *([cache_kpe] * npg))


def _qlat_kernel(q_ref, w_ref, g_ref, o_ref, *, scale):
    q = (q_ref[...].astype(F32) * g_ref[...] * scale).astype(BF16)
    r = lax.dot_general(q, w_ref[...], NT_DIMS, preferred_element_type=F32)
    o_ref[...] = r.reshape(o_ref.shape).astype(o_ref.dtype)


def absorbed_queries(c, q_s, w_uk, g_n, nb, nq):
    ts = nb * nq
    nope_blocks = c.hp // c.nope_hd
    return pl.pallas_call(
        functools.partial(_qlat_kernel, scale=c.qk_hd ** -0.5), grid=(c.mla_heads,),
        in_specs=[pl.BlockSpec((ts, c.nope_hd), lambda h: (0, h * nope_blocks)),
                  pl.BlockSpec((c.kv_lora, c.nope_hd), lambda h: (0, h)),
                  pl.BlockSpec((1, c.nope_hd), lambda h: (0, 0))],
        out_specs=pl.BlockSpec((nb, None, nq, c.kv_lora), lambda h: (0, h, 0, 0)),
        out_shape=jax.ShapeDtypeStruct((nb, c.mla_heads, nq, c.kv_lora), BF16),
        compiler_params=_params(("parallel",)), name="absorbed_queries",
    )(q_s, w_uk, g_n)


def _latent_out_kernel(ol_ref, w_ref, g_ref, o_ref):
    ol = ol_ref[...]
    ol = ol.reshape(ol.shape[0] * ol.shape[1], ol.shape[2]).astype(BF16)
    o = jnp.dot(ol, w_ref[...], preferred_element_type=F32)
    o_ref[...] = (o * _silu(g_ref[...])).astype(o_ref.dtype)


def latent_out(c, o_lat, w_uv, z2, row_blk):
    nb, heads, nq, _ = o_lat.shape
    ts = nb * nq
    return pl.pallas_call(
        _latent_out_kernel, grid=(heads,),
        in_specs=[pl.BlockSpec((nb, None, nq, c.kv_lora), lambda h: (0, h, 0, 0)),
                  pl.BlockSpec((c.kv_lora, c.v_hd), lambda h: (0, h)),
                  pl.BlockSpec((ts, c.v_hd), lambda h: (row_blk, h))],
        out_specs=pl.BlockSpec((ts, c.v_hd), lambda h: (0, h)),
        out_shape=jax.ShapeDtypeStruct((ts, c.branch_w), BF16),
        compiler_params=_params(("parallel",)), name="latent_out",
    )(o_lat, w_uv, z2)


def _gla_kernel(q_ref, k_ref, v_ref, slab_ref, wg_ref, bg_ref, gb_ref, og_ref, *rest, c, chunk, has_s0):
    if has_s0:
        s0_ref, o_ref, sout_ref, s_scr = rest
    else:
        o_ref, sout_ref, s_scr = rest
    ci = pl.program_id(2)

    @pl.when(ci == 0)
    def _():
        if has_s0:
            s_scr[...] = s0_ref[...]
        else:
            s_scr[...] = jnp.zeros(s_scr.shape, F32)

    khd, vhd = c.gla_khd, c.gla_vhd
    x = jnp.dot(slab_ref[...].astype(BF16), wg_ref[...], preferred_element_type=F32) + bg_ref[...]
    logf = (jnp.minimum(x, 0.0) - jnp.log(1.0 + jnp.exp(-jnp.abs(x)))) / c.gate_tau
    row = lax.broadcasted_iota(jnp.int32, (chunk, chunk), 0)
    col = lax.broadcasted_iota(jnp.int32, (chunk, chunk), 1)
    tril = (col <= row).astype(F32)
    cum = jnp.dot(tril, logf, preferred_element_type=F32, precision=lax.Precision.HIGHEST)
    last = cum[chunk - 1:chunk, :]
    q = q_ref[...] * khd ** -0.5
    k = k_ref[...]
    v = v_ref[...]
    s_prev = s_scr[...]
    inter = jnp.dot(q * jnp.exp(cum), s_prev, preferred_element_type=F32)
    a = jnp.zeros((chunk, chunk), F32)
    for s in range(chunk):
        t0 = (s // SUBLANES) * SUBLANES
        e = jnp.exp(cum[t0:, :] - cum[s:s + 1, :])
        colv = jnp.sum(q[t0:, :] * (k[s:s + 1, :] * e), axis=-1, keepdims=True)
        if t0:
            colv = jnp.concatenate([jnp.zeros((t0, 1), F32), colv], axis=0)
        a = jnp.where(col == s, colv, a)
    a = jnp.where(col <= row, a, 0.0)
    o = inter + jnp.dot(a, v, preferred_element_type=F32)
    ms = jnp.mean(o * o, axis=-1, keepdims=True)
    on = o * lax.rsqrt(ms + EPS) * og_ref[...]
    o_ref[...] = (on * _silu(gb_ref[...])).astype(o_ref.dtype)
    kt = k * jnp.exp(last - cum)
    upd = lax.dot_general(kt, v, TN_DIMS, preferred_element_type=F32)
    dcol = jnp.exp(jnp.transpose(jnp.broadcast_to(last, (LANES, khd))))
    for n in range(vhd // LANES):
        sl = slice(n * LANES, (n + 1) * LANES)
        s_scr[:, sl] = dcol * s_prev[:, sl] + upd[:, sl]

    @pl.when(ci == pl.num_programs(2) - 1)
    def _():
        sout_ref[...] = s_scr[...]


def gla(c, z3, z1, z2, w_gate_pad, b_gate, gla_o_g, s0, *, nb, length, row0):
    chunk = math.gcd(length, c.gla_chunk)
    nch = length // chunk
    assert row0 % chunk == 0
    heads, khd, vhd = c.gla_heads, c.gla_khd, c.gla_vhd
    base = row0 // chunk
    rmap = lambda off: (lambda b, h, ci: (base + b * nch + ci, off + h))
    slab_blk = (c.q_lora + c.kv_lora) // LANES
    in_specs = [pl.BlockSpec((chunk, khd), rmap(0)),
                pl.BlockSpec((chunk, khd), rmap(c.gla_kd // khd)),
                pl.BlockSpec((chunk, vhd), rmap(2 * c.gla_kd // vhd)),
                pl.BlockSpec((chunk, LANES), lambda b, h, ci: (base + b * nch + ci, slab_blk)),
                pl.BlockSpec((LANES, khd), lambda b, h, ci: (0, h)),
                pl.BlockSpec((1, khd), lambda b, h, ci: (0, h)),
                pl.BlockSpec((chunk, vhd), rmap(c.branch_w // vhd)),
                pl.BlockSpec((1, vhd), lambda b, h, ci: (0, 0))]
    args = [z3, z3, z3, z1, w_gate_pad, b_gate.reshape(1, -1), z2, gla_o_g.reshape(1, -1)]
    smap = lambda b, h, ci: (b, h, 0, 0)
    if s0 is not None:
        in_specs.append(pl.BlockSpec((None, None, khd, vhd), smap))
        args.append(s0)
    return pl.pallas_call(
        functools.partial(_gla_kernel, c=c, chunk=chunk, has_s0=s0 is not None), grid=(nb, heads, nch),
        in_specs=in_specs,
        out_specs=[pl.BlockSpec((chunk, vhd), lambda b, h, ci: (b * nch + ci, h)),
                   pl.BlockSpec((None, None, khd, vhd), smap)],
        out_shape=[jax.ShapeDtypeStruct((nb * length, c.branch_w), BF16),
                   jax.ShapeDtypeStruct((nb, heads, khd, vhd), F32)],
        scratch_shapes=[pltpu.VMEM((khd, vhd), F32)],
        compiler_params=_params(("parallel", "parallel", "arbitrary")), name="gla",
    )(*args)


def _mem_attn_kernel(q_ref, g_ref, k_ref, v_ref, gc_ref, o_ref, *, scale):
    q = q_ref[...]
    ms = jnp.mean(q * q, axis=-1, keepdims=True)
    qn = (q * lax.rsqrt(ms + EPS) * g_ref[...]).astype(BF16)
    s = lax.dot_general(qn, k_ref[...].astype(BF16), NT_DIMS, preferred_element_type=F32) * scale
    m = jnp.max(s, axis=-1, keepdims=True)
    p = jnp.exp(s - m)
    p = p / jnp.sum(p, axis=-1, keepdims=True)
    o = jnp.dot(p.astype(BF16), v_ref[...].astype(BF16), preferred_element_type=F32)
    o_ref[...] = (o * _silu(gc_ref[...])).astype(o_ref.dtype)


def mem_attend(c, z4, z2, mem_q_g, mem_k, mem_v, *, nb, length, row0, tq):
    tq = min(tq, length)
    nqt = length // tq
    assert row0 % tq == 0
    base = row0 // tq
    hd, n_mem = c.mem_hd, mem_k.shape[1]
    gate_off = 2 * c.branch_w // hd
    return pl.pallas_call(
        functools.partial(_mem_attn_kernel, scale=hd ** -0.5), grid=(nb, c.mem_heads, nqt),
        in_specs=[pl.BlockSpec((tq, hd), lambda b, h, i: (base + b * nqt + i, h)),
                  pl.BlockSpec((1, hd), lambda b, h, i: (0, 0)),
                  pl.BlockSpec((None, n_mem, hd), lambda b, h, i: (b, 0, h)),
                  pl.BlockSpec((None, n_mem, hd), lambda b, h, i: (b, 0, h)),
                  pl.BlockSpec((tq, hd), lambda b, h, i: (base + b * nqt + i, gate_off + h))],
        out_specs=pl.BlockSpec((tq, hd), lambda b, h, i: (b * nqt + i, h)),
        out_shape=jax.ShapeDtypeStruct((nb * length, c.branch_w), BF16),
        compiler_params=_params(("parallel", "parallel", "arbitrary")), name="mem_attend",
    )(z4, mem_q_g.reshape(1, -1), mem_k, mem_v, z2)


def _merge_kernel(ba_ref, bb_ref, bc_ref, w_ref, la_ref, lb_ref, lc_ref, bm_ref, o_ref):
    acc = None
    for n, (b_ref, l_ref) in enumerate(((ba_ref, la_ref), (bb_ref, lb_ref), (bc_ref, lc_ref))):
        proj = jnp.dot(b_ref[...], w_ref[n], preferred_element_type=F32)
        term = _sigmoid(l_ref[...] + bm_ref[n:n + 1, :]) * proj
        acc = term if acc is None else acc + term
    o_ref[...] = acc.astype(o_ref.dtype)


def merge(c, br_a, br_b, br_c, w_branch, z5, b_merge):
    t = br_a.shape[0]
    d = c.d_model
    tm, tn = min(c.tm_merge, t), min(c.tn_merge, d)
    nj = d // tn
    bmap = lambda i, j: (i, 0)
    lmap = lambda n: (lambda i, j: (i, n * nj + j))
    return pl.pallas_call(
        _merge_kernel, grid=(t // tm, nj),
        in_specs=[pl.BlockSpec((tm, c.branch_w), bmap)] * 3
        + [pl.BlockSpec((c.n_branch, c.branch_w, tn), lambda i, j: (0, 0, j))]
        + [pl.BlockSpec((tm, tn), lmap(n)) for n in range(3)]
        + [pl.BlockSpec((c.n_branch, tn), lambda i, j: (0, j))],
        out_specs=pl.BlockSpec((tm, tn), lambda i, j: (i, j)),
        out_shape=jax.ShapeDtypeStruct((t, d), BF16),
        compiler_params=_params(("parallel", "parallel")), name="merge",
    )(br_a, br_b, br_c, w_branch, z5, z5, z5, b_merge)


def _rope_tables(c, pos):
    half = c.rope_hd // 2
    inv = jnp.power(c.rope_theta, -jnp.arange(half, dtype=F32) / half)
    ang = pos.astype(F32)[:, None] * inv[None, :]
    cos, sin = jnp.cos(ang), jnp.sin(ang)
    z = jnp.zeros((pos.shape[0], LANES - c.rope_hd), F32)
    zh = jnp.zeros_like(cos)
    slab_tabs = (jnp.concatenate([cos, cos, z], 1), jnp.concatenate([-sin, zh, z], 1),
                 jnp.concatenate([zh, sin, z], 1))
    return slab_tabs, jnp.concatenate([cos, cos], 1), jnp.concatenate([sin, -sin], 1)


def _layer_weights(c, l, w):
    offs = np.concatenate([[0], np.cumsum(c.in_sizes)]).astype(int)
    seg = lambda i: w["w_in"][l][:, offs[i]:offs[i + 1]]
    d = c.d_model
    pad_cols = LANES - c.rope_hd - c.gate_rank
    o = SimpleNamespace()
    o.w1 = jnp.concatenate([seg(0), seg(1), seg(2), seg(7), jnp.zeros((d, pad_cols), F32)], 1).astype(BF16)
    o.w2 = jnp.concatenate([seg(3), seg(8), seg(10)], 1).astype(BF16)
    o.w3 = jnp.concatenate([seg(4), seg(5), seg(6)], 1).astype(BF16)
    o.w4 = seg(9).astype(BF16)
    o.w5 = seg(11).astype(BF16)
    heads = c.mla_heads
    wq = w["w_uq"][l].reshape(c.q_lora, heads, c.qk_hd)
    o.w_uq = jnp.pad(wq, ((0, 0), (0, 0), (0, c.hp - c.qk_hd))).reshape(c.q_lora, heads * c.hp).astype(BF16)
    padg = lambda g: jnp.pad(g, (0, c.hp - c.qk_hd)).reshape(1, c.hp)
    o.qg, o.kg = padg(w["mla_q_g"][l]), padg(w["mla_k_g"][l])
    o.w_uk = w["w_uk"][l].astype(BF16)
    o.w_ukt = w["w_uk"][l].T.astype(BF16)
    o.w_uv = w["w_uv"][l].astype(BF16)
    lo = c.rope_hd
    o.w_gate = jnp.zeros((LANES, c.gla_kd), F32).at[lo:lo + c.gate_rank].set(w["w_gate2"][l]).astype(BF16)
    o.w_mk, o.w_mv = w["w_mk"][l].astype(BF16), w["w_mv"][l].astype(BF16)
    o.w_branch = w["w_branch"][l].astype(BF16)
    o.w_out = w["w_out"][l].astype(BF16)
    return o


def _forward(c, x_prompt, x_sample, mem_prompt, cache_ckv, cache_kpe, cache_mem_k, cache_mem_v, state_gla,
             page_table, **w):
    c = _derive(c)
    batch, seq, d = x_prompt.shape
    nb, nq = x_sample.shape[:2]
    n_mem = mem_prompt.shape[1]
    depth = w["w_in"].shape[0]
    tp, ts = batch * seq, nb * nq
    past = page_table.shape[1] * c.page_size
    assert tp % ts == 0
    heads = c.mla_heads

    pos_all = jnp.concatenate([jnp.tile(jnp.arange(seq, dtype=jnp.int32), batch),
                               jnp.tile(past + jnp.arange(nq, dtype=jnp.int32), nb)])
    tabs, _, _ = _rope_tables(c, pos_all)
    _, ctab, stab = _rope_tables(c, jnp.arange(past + c.page_size, dtype=jnp.int32))

    x_all = jnp.concatenate([x_prompt.reshape(tp, d), x_sample.reshape(ts, d)], axis=0)
    mem_flat = mem_prompt.reshape(batch * n_mem, d)
    outs = {k: [] for k in ("ckv_p", "kpe_p", "gla_p", "mk_p", "mv_p", "ckv_s", "kpe_s", "gla_s")}

    for l in range(depth):
        lw = _layer_weights(c, l, w)
        h = rms_rows(x_all, w["norm_w"][l], c.tm_rows)
        mm = functools.partial(matmul, tm=c.tm_mm, tn=c.tn_mm)
        z1 = matmul(h, lw.w1, tm=c.tm_mm, tn=lw.w1.shape[1], name="in_proj_lat")
        z2, z3, z4, z5 = (mm(h, wg, name=nm) for wg, nm in
                          ((lw.w2, "in_proj_gate"), (lw.w3, "in_proj_gla"), (lw.w4, "in_proj_memq"),
                           (lw.w5, "in_proj_logit")))
        m_n = rms_rows(mem_flat, w["mem_norm_w"][l], c.tm_rows)
        mk = matmul(m_n, lw.w_mk, tm=c.tm_mm, tn=c.mem_hd, mode="tile_norm", extra=w["mem_k_g"][l],
                    name="mem_k")
        mv = mm(m_n, lw.w_mv, name="mem_v")
        q_full = q_prep(c, z1, w["q_lora_g"][l], lw.w_uq, lw.qg, tabs)
        ckv_n, kpe, k_full, v_full = kv_prep(c, z1, w["kv_lora_g"][l], lw.w_uk, lw.w_uv, lw.kg, tabs)
        br_a_p = flash_prompt(c, q_full, k_full, v_full, z2, batch, seq)
        q_s = q_full[tp:]
        g_n = w["mla_k_g"][l][:c.nope_hd].reshape(1, -1)
        g_p = w["mla_k_g"][l][c.nope_hd:]
        qlat = absorbed_queries(c, q_s, lw.w_uk, g_n, nb, nq).reshape(nb, heads * nq, c.kv_lora)
        q_pe = q_s.reshape(nb, nq, heads, c.hp)[..., c.nope_hd:c.qk_hd].astype(F32)
        q_pe = jnp.swapaxes(q_pe, 1, 2).reshape(nb, heads * nq, c.rope_hd) * c.qk_hd ** -0.5
        half = c.rope_hd // 2
        qpa = (q_pe * g_p).astype(BF16)
        qpb = (jnp.concatenate([q_pe[..., half:], q_pe[..., :half]], -1) * g_p).astype(BF16)
        c_new = ckv_n[tp:].reshape(nb, nq, c.kv_lora)
        kp_new = kpe[tp:].reshape(nb, nq, c.rope_hd)
        o_lat = paged_sample_attend(c, l, page_table, lw.w_ukt, qlat, qpa, qpb, c_new, kp_new, ctab, stab,
                                    cache_ckv, cache_kpe)
        br_a_s = latent_out(c, o_lat.reshape(nb, heads, nq, c.kv_lora), lw.w_uv, z2, tp // ts)
        br_b_p, gla_p = gla(c, z3, z1, z2, lw.w_gate, w["b_gate"][l], w["gla_o_g"][l], None,
                            nb=batch, length=seq, row0=0)
        br_b_s, gla_s = gla(c, z3, z1, z2, lw.w_gate, w["b_gate"][l], w["gla_o_g"][l], state_gla[l],
                            nb=nb, length=nq, row0=tp)
        br_c_p = mem_attend(c, z4, z2, w["mem_q_g"][l], mk.reshape(batch, n_mem, -1),
                            mv.reshape(batch, n_mem, -1), nb=batch, length=seq, row0=0, tq=c.tq_mem)
        br_c_s = mem_attend(c, z4, z2, w["mem_q_g"][l], cache_mem_k[l].reshape(nb, n_mem, -1),
                            cache_mem_v[l].reshape(nb, n_mem, -1), nb=nb, length=nq, row0=tp, tq=nq)
        cat = lambda p, s: jnp.concatenate([p, s], axis=0)
        mix = merge(c, cat(br_a_p, br_a_s), cat(br_b_p, br_b_s), cat(br_c_p, br_c_s), lw.w_branch, z5,
                    w["b_merge"][l])
        x_all = matmul(mix, lw.w_out, tm=c.tm_mm, tn=c.tn_mm, mode="residual", extra=x_all, name="out_proj")

        outs["ckv_p"].append(ckv_n[:tp].reshape(batch, seq, -1))
        outs["kpe_p"].append(kpe[:tp].reshape(batch, seq, -1))
        outs["gla_p"].append(gla_p)
        outs["mk_p"].append(mk.reshape(batch, n_mem, c.mem_heads, c.mem_hd))
        outs["mv_p"].append(mv.reshape(batch, n_mem, c.mem_heads, c.mem_hd))
        outs["ckv_s"].append(c_new)
        outs["kpe_s"].append(kp_new)
        outs["gla_s"].append(gla_s)

    st = lambda k: jnp.stack(outs[k])
    return (x_all[:tp].reshape(batch, seq, d), x_all[tp:].reshape(nb, nq, d),
            st("ckv_p"), st("kpe_p"), st("gla_p"), st("mk_p"), st("mv_p"), st("ckv_s"), st("kpe_s"), st("gla_s"))


def kernel(x_prompt, x_sample, mem_prompt, cache_ckv, cache_kpe, cache_mem_k, cache_mem_v, state_gla, page_table, norm_w, w_in, q_lora_g, kv_lora_g, w_uq, mla_q_g, mla_k_g, w_uk, w_uv, w_gate2, b_gate, gla_o_g, mem_norm_w, w_mk, w_mv, mem_q_g, mem_k_g, w_branch, b_merge, w_out):
    return _forward(default_config(), x_prompt, x_sample, mem_prompt, cache_ckv, cache_kpe, cache_mem_k,
                    cache_mem_v, state_gla, page_table,
                    norm_w=norm_w, w_in=w_in, q_lora_g=q_lora_g, kv_lora_g=kv_lora_g, w_uq=w_uq,
                    mla_q_g=mla_q_g, mla_k_g=mla_k_g, w_uk=w_uk, w_uv=w_uv, w_gate2=w_gate2, b_gate=b_gate,
                    gla_o_g=gla_o_g, mem_norm_w=mem_norm_w, w_mk=w_mk, w_mv=w_mv, mem_q_g=mem_q_g,
                    mem_k_g=mem_k_g, w_branch=w_branch, b_merge=b_merge, w_out=w_out)
```

```python
import functools
import math
from types import SimpleNamespace

import jax
import jax.numpy as jnp
import numpy as np
from jax import lax
from jax.experimental import pallas as pl
from jax.experimental.pallas import tpu as pltpu

F32 = jnp.float32
BF16 = jnp.bfloat16
EPS = 1e-6
LANES = 128
SUBLANES = 8
VMEM_LIMIT = 56 * 1024 * 1024

OFFSET_UNIT = 16

NT_DIMS = (((1,), (1,)), ((), ()))
TN_DIMS = (((0,), (0,)), ((), ()))

IN_NAMES = ("cq", "ckv", "kpe", "g_a", "gq", "gk", "gv", "g_lr", "g_b", "mq", "g_c", "m_logit")
Z_NAMES = ("cq", "ckv", "g_a", "gq", "gk", "gv", "g_b", "mq", "g_c", "m_logit")


def default_config():
    d_model = 4096
    branch_w = d_model // 2
    return SimpleNamespace(
        d_model=d_model, branch_w=branch_w, n_branch=3,
        v_hd=128, nope_hd=128, rope_hd=64, q_lora=d_model // 4, kv_lora=512,
        rope_theta=10000.0,
        gla_heads=4, gla_kd=d_model // 4, gla_vd=branch_w, gate_rank=16, gate_tau=16.0,
        gla_chunk=64,
        mem_heads=4, page_size=128,
        tm_mm=1024, tn_mm=1024, tn_in=512, tm_rows=512, tm_prep=256, tq_flash=512, tq_mem=512,
        pages_per_step=16, tm_merge=512, tn_merge=512,
    )


def _derive(c):
    c.mla_heads = c.branch_w // c.v_hd
    c.qk_hd = c.nope_hd + c.rope_hd
    c.hp = c.nope_hd + LANES
    c.gla_khd = c.gla_kd // c.gla_heads
    c.gla_vhd = c.gla_vd // c.gla_heads
    c.mem_hd = c.branch_w // c.mem_heads
    sizes = (c.q_lora, c.kv_lora, c.rope_hd, c.branch_w, c.gla_kd, c.gla_kd, c.gla_vd,
             c.gate_rank, c.branch_w, c.branch_w, c.branch_w, c.n_branch * c.d_model)
    c.in_size = dict(zip(IN_NAMES, sizes))
    c.in_off = dict(zip(IN_NAMES, np.concatenate([[0], np.cumsum(sizes)[:-1]]).astype(int).tolist()))
    zo, acc = {}, 0
    for n in Z_NAMES:
        zo[n] = acc
        acc += c.in_size[n]
    c.z_off, c.z_cols = zo, acc
    assert c.rope_hd + c.gate_rank <= LANES and c.nope_hd % LANES == 0
    assert all(c.in_size[n] % c.tn_in == 0 for n in Z_NAMES)
    return c


def _zblk(c, name, width):
    off = c.z_off[name]
    assert off % width == 0, (name, off, width)
    return off // width


def _params(sem):
    return pltpu.CompilerParams(dimension_semantics=sem, vmem_limit_bytes=VMEM_LIMIT)


def _silu(g):
    return g / (1.0 + jnp.exp(-g))


def _sigmoid(x):
    return 1.0 / (1.0 + jnp.exp(-x))


def _alias_prev(in_specs, args, prev, out_index):
    if prev is None:
        return {}
    in_specs.append(pl.BlockSpec(memory_space=pl.ANY))
    args.append(prev)
    return {len(args) - 1: out_index}


def _rms_rows_kernel(x_ref, g_ref, o_ref):
    x = x_ref[...]
    ms = jnp.mean(x * x, axis=-1, keepdims=True)
    o_ref[...] = (x * lax.rsqrt(ms + EPS) * g_ref[...]).astype(o_ref.dtype)


def rms_rows(x, g, tm):
    t, d = x.shape
    tm = min(tm, t)
    return pl.pallas_call(
        _rms_rows_kernel, grid=(t // tm,),
        in_specs=[pl.BlockSpec((tm, d), lambda i: (i, 0)), pl.BlockSpec((1, d), lambda i: (0, 0))],
        out_specs=pl.BlockSpec((tm, d), lambda i: (i, 0)),
        out_shape=jax.ShapeDtypeStruct((t, d), BF16),
        compiler_params=_params(("parallel",)), name="rms_rows",
    )(x, g.reshape(1, d))


def _mm_kernel(a_ref, b_ref, *rest, mode):
    acc = jnp.dot(a_ref[...].astype(BF16), b_ref[...].astype(BF16), preferred_element_type=F32)
    if mode == "plain":
        (o_ref,) = rest
    elif mode == "tile_norm":
        g_ref, o_ref = rest
        ms = jnp.mean(acc * acc, axis=-1, keepdims=True)
        acc = acc * lax.rsqrt(ms + EPS) * g_ref[...]
    elif mode == "residual":
        x_ref, o_ref = rest
        acc = x_ref[...] + acc
    o_ref[...] = acc.astype(o_ref.dtype)


def matmul(a, b, *, tm, tn, out_dtype=F32, mode="plain", extra=None, layer=None, name="matmul"):
    m, k = a.shape
    n = b.shape[-1]
    tm, tn = min(tm, m), min(tn, n)
    assert m % tm == 0 and n % tn == 0, (m, n, tm, tn)
    if layer is None:
        b_spec = pl.BlockSpec((k, tn), lambda i, j: (0, j))
    else:
        b_spec = pl.BlockSpec((None, k, tn), lambda i, j: (layer, 0, j))
    in_specs = [pl.BlockSpec((tm, k), lambda i, j: (i, 0)), b_spec]
    args = [a, b]
    if mode == "tile_norm":
        in_specs.append(pl.BlockSpec((1, tn), lambda i, j: (0, 0)))
        args.append(extra.reshape(1, tn))
    elif mode == "residual":
        in_specs.append(pl.BlockSpec((tm, tn), lambda i, j: (i, j)))
        args.append(extra)
    return pl.pallas_call(
        functools.partial(_mm_kernel, mode=mode), grid=(m // tm, n // tn),
        in_specs=in_specs, out_specs=pl.BlockSpec((tm, tn), lambda i, j: (i, j)),
        out_shape=jax.ShapeDtypeStruct((m, n), out_dtype),
        compiler_params=_params(("parallel", "parallel")), name=name,
    )(*args)


def _in_proj_kernel(offs_ref, h_ref, wt_ref, wk_ref, wg_ref, o_ref, os_ref):
    del offs_ref
    h = h_ref[...]
    w = wt_ref[...].astype(BF16)
    o_ref[...] = lax.dot_general(h, w, NT_DIMS, preferred_element_type=F32)

    @pl.when(pl.program_id(1) == 0)
    def _():
        pad = LANES - wk_ref.shape[0] - wg_ref.shape[0]
        ws = jnp.concatenate([wk_ref[...], wg_ref[...], jnp.zeros((pad, h.shape[1]), F32)], axis=0)
        os_ref[...] = lax.dot_general(h, ws.astype(BF16), NT_DIMS, preferred_element_type=F32)


def in_proj(c, h, w_in_t, layer, row_offsets, *, tm, tn):
    t, d = h.shape
    tm = min(tm, t)
    nt = row_offsets.shape[0]
    small = lambda name: pl.BlockSpec((None, pl.Element(c.in_size[name]), pl.Element(d)),
                                      lambda i, j, offs: (layer, c.in_off[name], 0))
    grid_spec = pltpu.PrefetchScalarGridSpec(
        num_scalar_prefetch=1, grid=(t // tm, nt),
        in_specs=[pl.BlockSpec((tm, d), lambda i, j, offs: (i, 0)),
                  pl.BlockSpec((None, pl.Element(tn), pl.Element(d)),
                               lambda i, j, offs: (layer, offs[j] * OFFSET_UNIT, 0)),
                  small("kpe"), small("g_lr")],
        out_specs=[pl.BlockSpec((tm, tn), lambda i, j, offs: (i, j)),
                   pl.BlockSpec((tm, LANES), lambda i, j, offs: (i, 0))])
    return pl.pallas_call(
        _in_proj_kernel, grid_spec=grid_spec,
        out_shape=[jax.ShapeDtypeStruct((t, nt * tn), F32), jax.ShapeDtypeStruct((t, LANES), F32)],
        compiler_params=_params(("parallel", "arbitrary")), name="in_proj",
    )(row_offsets, h, w_in_t, w_in_t, w_in_t)


def _rope_slab(x, cos, sin_lo, sin_hi, half):
    return x * cos + pltpu.roll(x, LANES - half, 1) * sin_lo + pltpu.roll(x, half, 1) * sin_hi


def _q_prep_kernel(cq_ref, gl_ref, w_ref, gq_ref, cos_ref, slo_ref, shi_ref, o_ref, *, c):
    cq = cq_ref[...]
    ms = jnp.mean(cq * cq, axis=-1, keepdims=True)
    cqn = (cq * lax.rsqrt(ms + EPS) * gl_ref[...]).astype(BF16)
    q = jnp.dot(cqn, w_ref[...], preferred_element_type=F32)
    cos, slo, shi = cos_ref[...], slo_ref[...], shi_ref[...]
    g_n, g_p = gq_ref[:, :c.nope_hd], gq_ref[:, c.nope_hd:]
    for h in range(c.mla_heads):
        lo = h * c.hp
        qn = q[:, lo:lo + c.nope_hd]
        qp = q[:, lo + c.nope_hd:lo + c.hp]
        ss = jnp.sum(qn * qn, axis=-1, keepdims=True) + jnp.sum(qp * qp, axis=-1, keepdims=True)
        r = lax.rsqrt(ss / c.qk_hd + EPS)
        o_ref[:, lo:lo + c.nope_hd] = (qn * r * g_n).astype(o_ref.dtype)
        roped = _rope_slab(qp * r * g_p, cos, slo, shi, c.rope_hd // 2)
        o_ref[:, lo + c.nope_hd:lo + c.hp] = roped.astype(o_ref.dtype)


def q_prep(c, z, q_lora_g, w_uq_pad, qg_pad, tabs):
    t = z.shape[0]
    tm = min(c.tm_prep, t)
    n = c.mla_heads * c.hp
    row = lambda i: (i, 0)
    const = lambda i: (0, 0)
    cq_blk = _zblk(c, "cq", c.q_lora)
    return pl.pallas_call(
        functools.partial(_q_prep_kernel, c=c), grid=(t // tm,),
        in_specs=[pl.BlockSpec((tm, c.q_lora), lambda i: (i, cq_blk)), pl.BlockSpec((1, c.q_lora), const),
                  pl.BlockSpec((c.q_lora, n), const), pl.BlockSpec((1, c.hp), const),
                  pl.BlockSpec((tm, LANES), row), pl.BlockSpec((tm, LANES), row),
                  pl.BlockSpec((tm, LANES), row)],
        out_specs=pl.BlockSpec((tm, n), row),
        out_shape=jax.ShapeDtypeStruct((t, n), BF16),
        compiler_params=_params(("parallel",)), name="q_prep",
    )(z, q_lora_g.reshape(1, -1), w_uq_pad, qg_pad, *tabs)


def _kv_prep_kernel(ckv_ref, slab_ref, gl_ref, wuk_ref, wuv_ref, kg_ref, cos_ref, slo_ref, shi_ref,
                    ckvn_ref, kpe_ref, k_ref, v_ref, *, c):
    ckv = ckv_ref[...]
    ms = jnp.mean(ckv * ckv, axis=-1, keepdims=True)
    cn = ckv * lax.rsqrt(ms + EPS) * gl_ref[...]
    ckvn_ref[...] = cn
    slab = slab_ref[...]
    kpe_ref[...] = slab[:, :c.rope_hd]
    lane = lax.broadcasted_iota(jnp.int32, slab.shape, 1)
    kp = jnp.where(lane < c.rope_hd, slab, 0.0)
    cb = cn.astype(BF16)
    kn = jnp.dot(cb, wuk_ref[...].astype(BF16), preferred_element_type=F32)
    v_ref[...] = jnp.dot(cb, wuv_ref[...].astype(BF16), preferred_element_type=F32).astype(v_ref.dtype)
    sspe = jnp.sum(kp * kp, axis=-1, keepdims=True)
    g_n, g_p = kg_ref[:, :c.nope_hd], kg_ref[:, c.nope_hd:]
    rp = _rope_slab(kp * g_p, cos_ref[...], slo_ref[...], shi_ref[...], c.rope_hd // 2)
    for h in range(c.mla_heads):
        knh = kn[:, h * c.nope_hd:(h + 1) * c.nope_hd]
        ss = jnp.sum(knh * knh, axis=-1, keepdims=True) + sspe
        r = lax.rsqrt(ss / c.qk_hd + EPS)
        lo = h * c.hp
        k_ref[:, lo:lo + c.nope_hd] = (knh * r * g_n).astype(k_ref.dtype)
        k_ref[:, lo + c.nope_hd:lo + c.hp] = (rp * r).astype(k_ref.dtype)


def kv_prep(c, z, zs, kv_lora_g, w_uk, w_uv, layer, kg_pad, tabs):
    t = z.shape[0]
    tm = min(c.tm_prep, t)
    row = lambda i: (i, 0)
    const = lambda i: (0, 0)
    wmap = lambda i: (layer, 0, 0)
    ckv_blk = _zblk(c, "ckv", c.kv_lora)
    nk, nv = c.mla_heads * c.hp, c.mla_heads * c.v_hd
    wshape = (None,) + w_uk.shape[1:]
    return pl.pallas_call(
        functools.partial(_kv_prep_kernel, c=c), grid=(t // tm,),
        in_specs=[pl.BlockSpec((tm, c.kv_lora), lambda i: (i, ckv_blk)),
                  pl.BlockSpec((tm, LANES), row),
                  pl.BlockSpec((1, c.kv_lora), const),
                  pl.BlockSpec(wshape, wmap), pl.BlockSpec(wshape, wmap),
                  pl.BlockSpec((1, c.hp), const),
                  pl.BlockSpec((tm, LANES), row), pl.BlockSpec((tm, LANES), row),
                  pl.BlockSpec((tm, LANES), row)],
        out_specs=[pl.BlockSpec((tm, c.kv_lora), row), pl.BlockSpec((tm, c.rope_hd), row),
                   pl.BlockSpec((tm, nk), row), pl.BlockSpec((tm, nv), row)],
        out_shape=[jax.ShapeDtypeStruct((t, c.kv_lora), F32), jax.ShapeDtypeStruct((t, c.rope_hd), F32),
                   jax.ShapeDtypeStruct((t, nk), BF16), jax.ShapeDtypeStruct((t, nv), BF16)],
        compiler_params=_params(("parallel",)), name="kv_prep",
    )(z, zs, kv_lora_g.reshape(1, -1), w_uk, w_uv, kg_pad, *tabs)


def _flash_kernel(qi_ref, ki_ref, q_ref, k_ref, v_ref, g_ref, o_ref, m_scr, l_scr, acc_scr, *, scale):
    step = pl.program_id(2)
    qi, ki = qi_ref[step], ki_ref[step]

    @pl.when(ki == 0)
    def _():
        m_scr[...] = jnp.full(m_scr.shape, -jnp.inf, F32)
        l_scr[...] = jnp.zeros(l_scr.shape, F32)
        acc_scr[...] = jnp.zeros(acc_scr.shape, F32)

    def update(diagonal):
        s = lax.dot_general(q_ref[...], k_ref[...], NT_DIMS, preferred_element_type=F32) * scale
        if diagonal:
            row = lax.broadcasted_iota(jnp.int32, s.shape, 0)
            col = lax.broadcasted_iota(jnp.int32, s.shape, 1)
            s = jnp.where(col <= row, s, -jnp.inf)
        m_prev = m_scr[...]
        m_new = jnp.maximum(m_prev, jnp.max(s, axis=-1, keepdims=True))
        alpha = jnp.exp(m_prev - m_new)
        p = jnp.exp(s - m_new)
        l_scr[...] = alpha * l_scr[...] + jnp.sum(p, axis=-1, keepdims=True)
        acc_scr[...] = alpha * acc_scr[...] + jnp.dot(p.astype(BF16), v_ref[...], preferred_element_type=F32)
        m_scr[...] = m_new

    @pl.when(ki < qi)
    def _():
        update(False)

    @pl.when(ki == qi)
    def _():
        update(True)
        o_ref[...] = (acc_scr[...] / l_scr[...] * _silu(g_ref[...])).astype(o_ref.dtype)


def flash_prompt(c, q_full, k_full, v_full, z, batch, seq):
    t = q_full.shape[0]
    tq = min(c.tq_flash, seq)
    nq = seq // tq
    pairs = [(qi, ki) for qi in range(nq) for ki in range(qi + 1)]
    qi_tab = jnp.asarray([p[0] for p in pairs], jnp.int32)
    ki_tab = jnp.asarray([p[1] for p in pairs], jnp.int32)
    ga_blk = _zblk(c, "g_a", c.v_hd)
    qmap = lambda b, h, s, qt, kt: (b * nq + qt[s], h)
    kmap = lambda b, h, s, qt, kt: (b * nq + kt[s], h)
    grid_spec = pltpu.PrefetchScalarGridSpec(
        num_scalar_prefetch=2, grid=(batch, c.mla_heads, len(pairs)),
        in_specs=[pl.BlockSpec((tq, c.hp), qmap), pl.BlockSpec((tq, c.hp), kmap),
                  pl.BlockSpec((tq, c.v_hd), kmap),
                  pl.BlockSpec((tq, c.v_hd), lambda b, h, s, qt, kt: (b * nq + qt[s], ga_blk + h))],
        out_specs=pl.BlockSpec((tq, c.v_hd), qmap),
        scratch_shapes=[pltpu.VMEM((tq, 1), F32), pltpu.VMEM((tq, 1), F32), pltpu.VMEM((tq, c.v_hd), F32)])
    return pl.pallas_call(
        functools.partial(_flash_kernel, scale=c.qk_hd ** -0.5), grid_spec=grid_spec,
        out_shape=jax.ShapeDtypeStruct((t, c.branch_w), BF16),
        compiler_params=_params(("parallel", "parallel", "arbitrary")), name="flash_prompt",
    )(qi_tab, ki_tab, q_full, k_full, v_full, z)


def _score_block(c, nq, lhs_ref, cb, kpt, ctab, stab, qpa, qpb):
    hn = c.mla_heads * c.nope_hd
    sspe = jnp.sum(kpt * kpt, axis=0, keepdims=True)
    res = lax.dot_general(lhs_ref[...], cb, NT_DIMS, preferred_element_type=F32)
    s = (res[hn:, :] + jnp.dot(qpa, (kpt * ctab).astype(BF16), preferred_element_type=F32)
         + jnp.dot(qpb, (kpt * stab).astype(BF16), preferred_element_type=F32))
    out = []
    for h in range(c.mla_heads):
        kh = res[h * c.nope_hd:(h + 1) * c.nope_hd, :]
        ss = jnp.sum(kh * kh, axis=0, keepdims=True) + sspe
        out.append(s[h * nq:(h + 1) * nq, :] * lax.rsqrt(ss / c.qk_hd + EPS))
    return out


def _paged_kernel(pt_ref, wukt_ref, qlat_ref, qpa_ref, qpb_ref, cnew_ref, kpnew_ref, ctab_ref, stab_ref,
                  ctabn_ref, stabn_ref, *rest, c, npages, nq):
    del pt_ref
    c_refs, kp_refs = rest[:npages], rest[npages:2 * npages]
    o_ref, lhs_scr, cbf_scr, s_scr, m_scr, l_scr, acc_scr = rest[2 * npages:]
    b, j = pl.program_id(0), pl.program_id(1)
    hn, hq = c.mla_heads * c.nope_hd, c.mla_heads * nq
    page = c.page_size

    @pl.when((b == 0) & (j == 0))
    def _():
        lhs_scr[0:hn, :] = wukt_ref[...]

    @pl.when(j == 0)
    def _():
        lhs_scr[hn:hn + hq, :] = qlat_ref[...]
        m_scr[...] = jnp.full(m_scr.shape, -jnp.inf, F32)
        l_scr[...] = jnp.zeros(l_scr.shape, F32)
        acc_scr[...] = jnp.zeros(acc_scr.shape, F32)

    qpa, qpb = qpa_ref[...], qpb_ref[...]

    def softmax_update(s, vals):
        m_prev = m_scr[...]
        m_new = jnp.maximum(m_prev, jnp.max(s, axis=-1, keepdims=True))
        alpha = jnp.exp(m_prev - m_new)
        p = jnp.exp(s - m_new)
        l_scr[...] = alpha * l_scr[...] + jnp.sum(p, axis=-1, keepdims=True)
        acc_scr[...] = alpha * acc_scr[...] + jnp.dot(p.astype(BF16), vals, preferred_element_type=F32)
        m_scr[...] = m_new

    for pp in range(npages // 2):
        rows = slice(pp * 2 * page, (pp + 1) * 2 * page)
        cb = jnp.concatenate([c_refs[2 * pp][...], c_refs[2 * pp + 1][...]], axis=0).astype(BF16)
        cbf_scr[rows, :] = cb
        kpt = jnp.concatenate([kp_refs[2 * pp][...], kp_refs[2 * pp + 1][...]], axis=1)
        sc = _score_block(c, nq, lhs_scr, cb, kpt, ctab_ref[:, rows], stab_ref[:, rows], qpa, qpb)
        for h in range(c.mla_heads):
            s_scr[h * nq:(h + 1) * nq, rows] = sc[h]
    softmax_update(s_scr[...], cbf_scr[...])

    @pl.when(j == pl.num_programs(1) - 1)
    def _():
        cb = jnp.concatenate([cnew_ref[...], jnp.zeros((page - nq, c.kv_lora), F32)], axis=0).astype(BF16)
        sc = _score_block(c, nq, lhs_scr, cb, kpnew_ref[...], ctabn_ref[...], stabn_ref[...], qpa, qpb)
        s = jnp.concatenate(sc, axis=0)
        col = lax.broadcasted_iota(jnp.int32, s.shape, 1)
        qidx = lax.broadcasted_iota(jnp.int32, s.shape, 0) % nq
        softmax_update(jnp.where(col <= qidx, s, -jnp.inf), cb)
        o_ref[...] = acc_scr[...] / l_scr[...]


def paged_sample_attend(c, layer, page_table, w_ukt, qlat, qpa, qpb, c_new, kpt_new, ctab_t, stab_t,
                        cache_ckv, cache_kpe_t):
    nb, n_pages = page_table.shape
    nq = c_new.shape[1]
    hq = c.mla_heads * nq
    hn = c.mla_heads * c.nope_hd
    npg = min(c.pages_per_step, n_pages)
    assert n_pages % npg == 0 and npg % 2 == 0
    steps = n_pages // npg
    rows = npg * c.page_size
    past = n_pages * c.page_size
    const2 = lambda b, j, pt: (0, 0)
    per_b = lambda b, j, pt: (b, 0, 0)
    tab_map = lambda b, j, pt: (0, j)
    in_specs = [pl.BlockSpec((hn, c.kv_lora), const2),
                pl.BlockSpec((None, hq, c.kv_lora), per_b),
                pl.BlockSpec((None, hq, c.rope_hd), per_b), pl.BlockSpec((None, hq, c.rope_hd), per_b),
                pl.BlockSpec((None, nq, c.kv_lora), per_b), pl.BlockSpec((None, c.rope_hd, c.page_size), per_b),
                pl.BlockSpec((c.rope_hd, rows), tab_map), pl.BlockSpec((c.rope_hd, rows), tab_map),
                pl.BlockSpec((c.rope_hd, c.page_size), const2), pl.BlockSpec((c.rope_hd, c.page_size), const2)]
    for i in range(npg):
        in_specs.append(pl.BlockSpec((None, None, c.page_size, c.kv_lora),
                                     lambda b, j, pt, i=i: (layer, pt[b, j * npg + i], 0, 0)))
    for i in range(npg):
        in_specs.append(pl.BlockSpec((None, None, c.rope_hd, c.page_size),
                                     lambda b, j, pt, i=i: (layer, pt[b, j * npg + i], 0, 0)))
    grid_spec = pltpu.PrefetchScalarGridSpec(
        num_scalar_prefetch=1, grid=(nb, steps), in_specs=in_specs,
        out_specs=pl.BlockSpec((None, hq, c.kv_lora), per_b),
        scratch_shapes=[pltpu.VMEM((hn + hq, c.kv_lora), BF16), pltpu.VMEM((rows, c.kv_lora), BF16),
                        pltpu.VMEM((hq, rows), F32), pltpu.VMEM((hq, 1), F32), pltpu.VMEM((hq, 1), F32),
                        pltpu.VMEM((hq, c.kv_lora), F32)])
    return pl.pallas_call(
        functools.partial(_paged_kernel, c=c, npages=npg, nq=nq), grid_spec=grid_spec,
        out_shape=jax.ShapeDtypeStruct((nb, hq, c.kv_lora), F32),
        compiler_params=_params(("arbitrary", "arbitrary")), name="paged_sample_attend",
    )(page_table, w_ukt, qlat, qpa, qpb, c_new, kpt_new, ctab_t[:, :past], stab_t[:, :past],
      ctab_t[:, past:], stab_t[:, past:], *([cache_ckv] * npg), *([cache_kpe_t] * npg))


def _qlat_kernel(q_ref, w_ref, g_ref, o_ref, *, scale):
    q = (q_ref[...].astype(F32) * g_ref[...] * scale).astype(BF16)
    r = lax.dot_general(q, w_ref[...].astype(BF16), NT_DIMS, preferred_element_type=F32)
    o_ref[...] = r.reshape(o_ref.shape).astype(o_ref.dtype)


def absorbed_queries(c, q_full, row_blk, w_uk, layer, g_n, nb, nq):
    ts = nb * nq
    nope_blocks = c.hp // c.nope_hd
    return pl.pallas_call(
        functools.partial(_qlat_kernel, scale=c.qk_hd ** -0.5), grid=(c.mla_heads,),
        in_specs=[pl.BlockSpec((ts, c.nope_hd), lambda h: (row_blk, h * nope_blocks)),
                  pl.BlockSpec((None, c.kv_lora, c.nope_hd), lambda h: (layer, 0, h)),
                  pl.BlockSpec((1, c.nope_hd), lambda h: (0, 0))],
        out_specs=pl.BlockSpec((nb, None, nq, c.kv_lora), lambda h: (0, h, 0, 0)),
        out_shape=jax.ShapeDtypeStruct((nb, c.mla_heads, nq, c.kv_lora), BF16),
        compiler_params=_params(("parallel",)), name="absorbed_queries",
    )(q_full, w_uk, g_n)


def _latent_out_kernel(ol_ref, w_ref, g_ref, *rest):
    o_ref = rest[-1]
    ol = ol_ref[...]
    ol = ol.reshape(ol.shape[0] * ol.shape[1], ol.shape[2]).astype(BF16)
    o = jnp.dot(ol, w_ref[...].astype(BF16), preferred_element_type=F32)
    o_ref[...] = (o * _silu(g_ref[...])).astype(o_ref.dtype)


def latent_out(c, o_lat, w_uv, layer, z, row_blk, prev):
    nb, heads, nq, _ = o_lat.shape
    ts = nb * nq
    ga_blk = _zblk(c, "g_a", c.v_hd)
    in_specs = [pl.BlockSpec((nb, None, nq, c.kv_lora), lambda h: (0, h, 0, 0)),
                pl.BlockSpec((None, c.kv_lora, c.v_hd), lambda h: (layer, 0, h)),
                pl.BlockSpec((ts, c.v_hd), lambda h: (row_blk, ga_blk + h))]
    args = [o_lat, w_uv, z]
    aliases = _alias_prev(in_specs, args, prev, 0)
    return pl.pallas_call(
        _latent_out_kernel, grid=(heads,), in_specs=in_specs,
        out_specs=pl.BlockSpec((ts, c.v_hd), lambda h: (row_blk, h)),
        out_shape=jax.ShapeDtypeStruct(prev.shape, prev.dtype), input_output_aliases=aliases,
        compiler_params=_params(("parallel",)), name="latent_out",
    )(*args)


def _gla_kernel(q_ref, k_ref, v_ref, slab_ref, wg_ref, bg_ref, gb_ref, og_ref, *rest, c, chunk, has_s0,
                n_alias):
    rest = list(rest)
    s0_ref = rest.pop(0) if has_s0 else None
    o_ref, sout_ref, s_scr = rest[n_alias:]
    ci = pl.program_id(2)

    @pl.when(ci == 0)
    def _():
        if has_s0:
            s_scr[...] = s0_ref[...]
        else:
            s_scr[...] = jnp.zeros(s_scr.shape, F32)

    khd, vhd = c.gla_khd, c.gla_vhd
    x = jnp.dot(slab_ref[...].astype(BF16), wg_ref[...], preferred_element_type=F32) + bg_ref[...]
    logf = (jnp.minimum(x, 0.0) - jnp.log(1.0 + jnp.exp(-jnp.abs(x)))) / c.gate_tau
    row = lax.broadcasted_iota(jnp.int32, (chunk, chunk), 0)
    col = lax.broadcasted_iota(jnp.int32, (chunk, chunk), 1)
    tril = (col <= row).astype(F32)
    cum = jnp.dot(tril, logf, preferred_element_type=F32, precision=lax.Precision.HIGHEST)
    last = cum[chunk - 1:chunk, :]
    q = q_ref[...] * khd ** -0.5
    k = k_ref[...]
    v = v_ref[...]
    s_prev = s_scr[...]
    inter = jnp.dot(q * jnp.exp(cum), s_prev, preferred_element_type=F32)
    a = jnp.zeros((chunk, chunk), F32)
    for s in range(chunk):
        t0 = (s // SUBLANES) * SUBLANES
        e = jnp.exp(cum[t0:, :] - cum[s:s + 1, :])
        colv = jnp.sum(q[t0:, :] * (k[s:s + 1, :] * e), axis=-1, keepdims=True)
        if t0:
            colv = jnp.concatenate([jnp.zeros((t0, 1), F32), colv], axis=0)
        a = jnp.where(col == s, colv, a)
    a = jnp.where(col <= row, a, 0.0)
    o = inter + jnp.dot(a, v, preferred_element_type=F32)
    ms = jnp.mean(o * o, axis=-1, keepdims=True)
    on = o * lax.rsqrt(ms + EPS) * og_ref[...]
    o_ref[...] = (on * _silu(gb_ref[...])).astype(o_ref.dtype)
    kt = k * jnp.exp(last - cum)
    upd = lax.dot_general(kt, v, TN_DIMS, preferred_element_type=F32)
    dcol = jnp.exp(jnp.transpose(jnp.broadcast_to(last, (LANES, khd))))
    for n in range(vhd // LANES):
        sl = slice(n * LANES, (n + 1) * LANES)
        s_scr[:, sl] = dcol * s_prev[:, sl] + upd[:, sl]

    @pl.when(ci == pl.num_programs(2) - 1)
    def _():
        sout_ref[...] = s_scr[...]


def gla(c, z, zs, w_gate_pad, b_gate, gla_o_g, s0, layer, depth, *, nb, length, row0, o_prev, s_prev):
    t = z.shape[0]
    chunk = math.gcd(length, c.gla_chunk)
    nch = length // chunk
    assert row0 % chunk == 0
    heads, khd, vhd = c.gla_heads, c.gla_khd, c.gla_vhd
    base = row0 // chunk
    rmap = lambda off: (lambda b, h, ci: (base + b * nch + ci, off + h))
    in_specs = [pl.BlockSpec((chunk, khd), rmap(_zblk(c, "gq", khd))),
                pl.BlockSpec((chunk, khd), rmap(_zblk(c, "gk", khd))),
                pl.BlockSpec((chunk, vhd), rmap(_zblk(c, "gv", vhd))),
                pl.BlockSpec((chunk, LANES), lambda b, h, ci: (base + b * nch + ci, 0)),
                pl.BlockSpec((LANES, khd), lambda b, h, ci: (0, h)),
                pl.BlockSpec((1, khd), lambda b, h, ci: (0, h)),
                pl.BlockSpec((chunk, vhd), rmap(_zblk(c, "g_b", vhd))),
                pl.BlockSpec((1, vhd), lambda b, h, ci: (0, 0))]
    args = [z, z, z, zs, w_gate_pad, b_gate.reshape(1, -1), z, gla_o_g.reshape(1, -1)]
    smap = lambda b, h, ci: (layer, b, h, 0, 0)
    s_blk = (None, None, None, khd, vhd)
    if s0 is not None:
        in_specs.append(pl.BlockSpec(s_blk, smap))
        args.append(s0)
    aliases = {}
    aliases.update(_alias_prev(in_specs, args, o_prev, 0))
    aliases.update(_alias_prev(in_specs, args, s_prev, 1))
    return pl.pallas_call(
        functools.partial(_gla_kernel, c=c, chunk=chunk, has_s0=s0 is not None, n_alias=len(aliases)),
        grid=(nb, heads, nch), in_specs=in_specs,
        out_specs=[pl.BlockSpec((chunk, vhd), rmap(0)), pl.BlockSpec(s_blk, smap)],
        out_shape=[jax.ShapeDtypeStruct((t, c.branch_w), BF16),
                   jax.ShapeDtypeStruct((depth, nb, heads, khd, vhd), F32)],
        scratch_shapes=[pltpu.VMEM((khd, vhd), F32)], input_output_aliases=aliases,
        compiler_params=_params(("parallel", "parallel", "arbitrary")), name="gla",
    )(*args)


def _mem_attn_kernel(q_ref, g_ref, k_ref, v_ref, gc_ref, *rest, scale):
    o_ref = rest[-1]
    q = q_ref[...]
    ms = jnp.mean(q * q, axis=-1, keepdims=True)
    qn = (q * lax.rsqrt(ms + EPS) * g_ref[...]).astype(BF16)
    s = lax.dot_general(qn, k_ref[...].astype(BF16), NT_DIMS, preferred_element_type=F32) * scale
    m = jnp.max(s, axis=-1, keepdims=True)
    p = jnp.exp(s - m)
    p = p / jnp.sum(p, axis=-1, keepdims=True)
    o = jnp.dot(p.astype(BF16), v_ref[...].astype(BF16), preferred_element_type=F32)
    o_ref[...] = (o * _silu(gc_ref[...])).astype(o_ref.dtype)


def mem_attend(c, z, mem_q_g, mem_k, mem_v, layer, *, nb, length, row0, tq, prev):
    t = z.shape[0]
    tq = min(tq, length)
    nqt = length // tq
    assert row0 % tq == 0
    base = row0 // tq
    hd, n_mem = c.mem_hd, mem_k.shape[2]
    q_blk, g_blk = _zblk(c, "mq", hd), _zblk(c, "g_c", hd)
    kv_spec = pl.BlockSpec((None, None, n_mem, hd), lambda b, h, i: (layer, b, 0, h))
    in_specs = [pl.BlockSpec((tq, hd), lambda b, h, i: (base + b * nqt + i, q_blk + h)),
                pl.BlockSpec((1, hd), lambda b, h, i: (0, 0)), kv_spec, kv_spec,
                pl.BlockSpec((tq, hd), lambda b, h, i: (base + b * nqt + i, g_blk + h))]
    args = [z, mem_q_g.reshape(1, -1), mem_k, mem_v, z]
    aliases = _alias_prev(in_specs, args, prev, 0)
    return pl.pallas_call(
        functools.partial(_mem_attn_kernel, scale=hd ** -0.5), grid=(nb, c.mem_heads, nqt),
        in_specs=in_specs,
        out_specs=pl.BlockSpec((tq, hd), lambda b, h, i: (base + b * nqt + i, h)),
        out_shape=jax.ShapeDtypeStruct((t, c.branch_w), BF16), input_output_aliases=aliases,
        compiler_params=_params(("parallel", "parallel", "arbitrary")), name="mem_attend",
    )(*args)


def _merge_kernel(ba_ref, bb_ref, bc_ref, w_ref, la_ref, lb_ref, lc_ref, bm_ref, o_ref):
    acc = None
    for n, (b_ref, l_ref) in enumerate(((ba_ref, la_ref), (bb_ref, lb_ref), (bc_ref, lc_ref))):
        proj = jnp.dot(b_ref[...], w_ref[n].astype(BF16), preferred_element_type=F32)
        term = _sigmoid(l_ref[...] + bm_ref[n:n + 1, :]) * proj
        acc = term if acc is None else acc + term
    o_ref[...] = acc.astype(o_ref.dtype)


def merge(c, br_a, br_b, br_c, w_branch, layer, z, b_merge):
    t = br_a.shape[0]
    d = c.d_model
    tm, tn = min(c.tm_merge, t), min(c.tn_merge, d)
    nj = d // tn
    l_blk = _zblk(c, "m_logit", tn)
    bmap = lambda j, i: (i, 0)
    lmap = lambda n: (lambda j, i: (i, l_blk + n * nj + j))
    return pl.pallas_call(
        _merge_kernel, grid=(nj, t // tm),
        in_specs=[pl.BlockSpec((tm, c.branch_w), bmap)] * 3
        + [pl.BlockSpec((None, c.n_branch, c.branch_w, tn), lambda j, i: (layer, 0, 0, j))]
        + [pl.BlockSpec((tm, tn), lmap(n)) for n in range(3)]
        + [pl.BlockSpec((c.n_branch, tn), lambda j, i: (0, j))],
        out_specs=pl.BlockSpec((tm, tn), lambda j, i: (i, j)),
        out_shape=jax.ShapeDtypeStruct((t, d), BF16),
        compiler_params=_params(("parallel", "parallel")), name="merge",
    )(br_a, br_b, br_c, w_branch, z, z, z, b_merge)


def _rope_tables(c, pos):
    half = c.rope_hd // 2
    inv = jnp.power(c.rope_theta, -jnp.arange(half, dtype=F32) / half)
    ang = pos.astype(F32)[:, None] * inv[None, :]
    cos, sin = jnp.cos(ang), jnp.sin(ang)
    z = jnp.zeros((pos.shape[0], LANES - c.rope_hd), F32)
    zh = jnp.zeros_like(cos)
    slab_tabs = (jnp.concatenate([cos, cos, z], 1), jnp.concatenate([-sin, zh, z], 1),
                 jnp.concatenate([zh, sin, z], 1))
    return slab_tabs, jnp.concatenate([cos, cos], 1).T, jnp.concatenate([sin, -sin], 1).T


def _in_proj_offsets(c):
    offs = []
    for n in Z_NAMES:
        offs += [c.in_off[n] + k * c.tn_in for k in range(c.in_size[n] // c.tn_in)]
    assert all(o % OFFSET_UNIT == 0 for o in offs)
    return jnp.asarray([o // OFFSET_UNIT for o in offs], jnp.int32)


def _forward(c, x_prompt, x_sample, mem_prompt, cache_ckv, cache_kpe, cache_mem_k, cache_mem_v, state_gla,
             page_table, **w):
    c = _derive(c)
    batch, seq, d = x_prompt.shape
    nb, nq = x_sample.shape[:2]
    n_mem = mem_prompt.shape[1]
    depth = w["w_in"].shape[0]
    tp, ts = batch * seq, nb * nq
    t = tp + ts
    past = page_table.shape[1] * c.page_size
    assert tp % ts == 0
    heads = c.mla_heads

    pos_all = jnp.concatenate([jnp.tile(jnp.arange(seq, dtype=jnp.int32), batch),
                               jnp.tile(past + jnp.arange(nq, dtype=jnp.int32), nb)])
    tabs, _, _ = _rope_tables(c, pos_all)
    _, ctab_t, stab_t = _rope_tables(c, jnp.arange(past + c.page_size, dtype=jnp.int32))

    x_all = jnp.concatenate([x_prompt.reshape(tp, d), x_sample.reshape(ts, d)], axis=0)
    mem_flat = mem_prompt.reshape(batch * n_mem, d)
    w_in_t = jnp.swapaxes(w["w_in"], 1, 2)
    cache_kpe_t = jnp.swapaxes(cache_kpe, 2, 3)
    mem_k_all = cache_mem_k.reshape(depth, nb, n_mem, -1)
    mem_v_all = cache_mem_v.reshape(depth, nb, n_mem, -1)
    in_offs = _in_proj_offsets(c)
    half = c.rope_hd // 2
    outs = {k: [] for k in ("ckv_p", "kpe_p", "gla_p", "mk_p", "mv_p", "ckv_s", "kpe_s")}
    gla_s = None

    for l in range(depth):
        wq = w["w_uq"][l].reshape(c.q_lora, heads, c.qk_hd)
        w_uq_pad = jnp.pad(wq, ((0, 0), (0, 0), (0, c.hp - c.qk_hd))).reshape(c.q_lora, heads * c.hp)
        w_uq_pad = w_uq_pad.astype(BF16)
        padg = lambda g: jnp.pad(g, (0, c.hp - c.qk_hd)).reshape(1, c.hp)
        qg_pad, kg_pad = padg(w["mla_q_g"][l]), padg(w["mla_k_g"][l])
        w_ukt = w["w_uk"][l].T.astype(BF16)
        w_gate = jnp.zeros((LANES, c.gla_kd), F32).at[c.rope_hd:c.rope_hd + c.gate_rank].set(w["w_gate2"][l])
        w_gate = w_gate.astype(BF16)

        h = rms_rows(x_all, w["norm_w"][l], c.tm_rows)
        z, zs = in_proj(c, h, w_in_t, l, in_offs, tm=c.tm_mm, tn=c.tn_in)
        m_n = rms_rows(mem_flat, w["mem_norm_w"][l], c.tm_rows)
        mk = matmul(m_n, w["w_mk"], tm=c.tm_mm, tn=c.mem_hd, mode="tile_norm", extra=w["mem_k_g"][l],
                    layer=l, name="mem_k")
        mv = matmul(m_n, w["w_mv"], tm=c.tm_mm, tn=c.tn_in, layer=l, name="mem_v")
        q_full = q_prep(c, z, w["q_lora_g"][l], w_uq_pad, qg_pad, tabs)
        ckv_n, kpe, k_full, v_full = kv_prep(c, z, zs, w["kv_lora_g"][l], w["w_uk"], w["w_uv"], l, kg_pad, tabs)
        br_a = flash_prompt(c, q_full, k_full, v_full, z, batch, seq)
        g_n = w["mla_k_g"][l][:c.nope_hd].reshape(1, -1)
        g_p = w["mla_k_g"][l][c.nope_hd:]
        qlat = absorbed_queries(c, q_full, tp // ts, w["w_uk"], l, g_n, nb, nq)
        qlat = qlat.reshape(nb, heads * nq, c.kv_lora)
        q_pe = q_full[tp:].reshape(nb, nq, heads, c.hp)[..., c.nope_hd:c.qk_hd].astype(F32)
        q_pe = jnp.swapaxes(q_pe, 1, 2).reshape(nb, heads * nq, c.rope_hd) * c.qk_hd ** -0.5
        qpa = (q_pe * g_p).astype(BF16)
        qpb = (jnp.concatenate([q_pe[..., half:], q_pe[..., :half]], -1) * g_p).astype(BF16)
        c_new = ckv_n[tp:].reshape(nb, nq, c.kv_lora)
        kp_new = kpe[tp:].reshape(nb, nq, c.rope_hd)
        kpt_new = jnp.pad(jnp.swapaxes(kp_new, 1, 2), ((0, 0), (0, 0), (0, c.page_size - nq)))
        o_lat = paged_sample_attend(c, l, page_table, w_ukt, qlat, qpa, qpb, c_new, kpt_new, ctab_t, stab_t,
                                    cache_ckv, cache_kpe_t)
        br_a = latent_out(c, o_lat.reshape(nb, heads, nq, c.kv_lora), w["w_uv"], l, z, tp // ts, br_a)
        br_b, gla_p = gla(c, z, zs, w_gate, w["b_gate"][l], w["gla_o_g"][l], None, 0, 1,
                          nb=batch, length=seq, row0=0, o_prev=None, s_prev=None)
        br_b, gla_s = gla(c, z, zs, w_gate, w["b_gate"][l], w["gla_o_g"][l], state_gla, l, depth,
                          nb=nb, length=nq, row0=tp, o_prev=br_b, s_prev=gla_s)
        br_c = mem_attend(c, z, w["mem_q_g"][l], mk.reshape(1, batch, n_mem, -1), mv.reshape(1, batch, n_mem, -1),
                          0, nb=batch, length=seq, row0=0, tq=c.tq_mem, prev=None)
        br_c = mem_attend(c, z, w["mem_q_g"][l], mem_k_all, mem_v_all, l, nb=nb, length=nq, row0=tp, tq=nq,
                          prev=br_c)
        mix = merge(c, br_a, br_b, br_c, w["w_branch"], l, z, w["b_merge"][l])
        x_all = matmul(mix, w["w_out"], tm=c.tm_mm, tn=c.tn_in, mode="residual", extra=x_all, layer=l,
                       name="out_proj")

        outs["ckv_p"].append(ckv_n[:tp].reshape(batch, seq, -1))
        outs["kpe_p"].append(kpe[:tp].reshape(batch, seq, -1))
        outs["gla_p"].append(gla_p[0])
        outs["mk_p"].append(mk.reshape(batch, n_mem, c.mem_heads, c.mem_hd))
        outs["mv_p"].append(mv.reshape(batch, n_mem, c.mem_heads, c.mem_hd))
        outs["ckv_s"].append(c_new)
        outs["kpe_s"].append(kp_new)

    st = lambda k: jnp.stack(outs[k])
    return (x_all[:tp].reshape(batch, seq, d), x_all[tp:].reshape(nb, nq, d),
            st("ckv_p"), st("kpe_p"), st("gla_p"), st("mk_p"), st("mv_p"), st("ckv_s"), st("kpe_s"), gla_s)


def kernel(x_prompt, x_sample, mem_prompt, cache_ckv, cache_kpe, cache_mem_k, cache_mem_v, state_gla, page_table, norm_w, w_in, q_lora_g, kv_lora_g, w_uq, mla_q_g, mla_k_g, w_uk, w_uv, w_gate2, b_gate, gla_o_g, mem_norm_w, w_mk, w_mv, mem_q_g, mem_k_g, w_branch, b_merge, w_out):
    return _forward(default_config(), x_prompt, x_sample, mem_prompt, cache_ckv, cache_kpe, cache_mem_k,
                    cache_mem_v, state_gla, page_table,
                    norm_w=norm_w, w_in=w_in, q_lora_g=q_lora_g, kv_lora_g=kv_lora_g, w_uq=w_uq,
                    mla_q_g=mla_q_g, mla_k_g=mla_k_g, w_uk=w_uk, w_uv=w_uv, w_gate2=w_gate2, b_gate=b_gate,
                    gla_o_g=gla_o_g, mem_norm_w=mem_norm_w, w_mk=w_mk, w_mv=w_mv, mem_q_g=mem_q_g,
                    mem_k_g=mem_k_g, w_branch=w_branch, b_merge=b_merge, w_out=w_out)
```

```python
import functools
import math
from types import SimpleNamespace

import jax
import jax.numpy as jnp
import numpy as np
from jax import lax
from jax.experimental import pallas as pl
from jax.experimental.pallas import tpu as pltpu

F32 = jnp.float32
BF16 = jnp.bfloat16
EPS = 1e-6
LANES = 128
SUBLANES = 8
VMEM_LIMIT = 56 * 1024 * 1024

OFFSET_UNIT = 16

NT_DIMS = (((1,), (1,)), ((), ()))
TN_DIMS = (((0,), (0,)), ((), ()))

IN_NAMES = ("cq", "ckv", "kpe", "g_a", "gq", "gk", "gv", "g_lr", "g_b", "mq", "g_c", "m_logit")
Z_NAMES = ("m_logit", "g_a", "g_b", "mq", "g_c", "gv", "gq", "gk", "cq", "ckv")


def default_config():
    d_model = 4096
    branch_w = d_model // 2
    return SimpleNamespace(
        d_model=d_model, branch_w=branch_w, n_branch=3,
        v_hd=128, nope_hd=128, rope_hd=64, q_lora=d_model // 4, kv_lora=512,
        rope_theta=10000.0,
        gla_heads=4, gla_kd=d_model // 4, gla_vd=branch_w, gate_rank=16, gate_tau=16.0,
        gla_chunk=64,
        mem_heads=4, page_size=128,
        tm_mm=1024, tn_mm=1024, tn_in=512, tm_rows=512, tm_prep=256, tq_flash=512, tq_mem=512,
        pages_per_step=16, tm_merge=512, tn_merge=512, flash_heads_per_step=2,
    )


def _derive(c):
    c.mla_heads = c.branch_w // c.v_hd
    c.qk_hd = c.nope_hd + c.rope_hd
    c.hp = c.nope_hd + LANES
    c.gla_khd = c.gla_kd // c.gla_heads
    c.gla_vhd = c.gla_vd // c.gla_heads
    c.mem_hd = c.branch_w // c.mem_heads
    sizes = (c.q_lora, c.kv_lora, c.rope_hd, c.branch_w, c.gla_kd, c.gla_kd, c.gla_vd,
             c.gate_rank, c.branch_w, c.branch_w, c.branch_w, c.n_branch * c.d_model)
    c.in_size = dict(zip(IN_NAMES, sizes))
    c.in_off = dict(zip(IN_NAMES, np.concatenate([[0], np.cumsum(sizes)[:-1]]).astype(int).tolist()))
    zo, acc = {}, 0
    for n in Z_NAMES:
        zo[n] = acc
        acc += c.in_size[n]
    c.z_off, c.z_cols = zo, acc
    assert c.rope_hd + c.gate_rank <= LANES and c.nope_hd % LANES == 0
    assert all(c.in_size[n] % c.tn_in == 0 for n in Z_NAMES)
    return c


def _zblk(c, name, width):
    off = c.z_off[name]
    assert off % width == 0, (name, off, width)
    return off // width


def _params(sem):
    return pltpu.CompilerParams(dimension_semantics=sem, vmem_limit_bytes=VMEM_LIMIT)


def _silu(g):
    return g / (1.0 + jnp.exp(-g))


def _sigmoid(x):
    return 1.0 / (1.0 + jnp.exp(-x))


def _alias_prev(in_specs, args, prev, out_index):
    if prev is None:
        return {}
    in_specs.append(pl.BlockSpec(memory_space=pl.ANY))
    args.append(prev)
    return {len(args) - 1: out_index}


def _rms_rows_kernel(x_ref, g_ref, o_ref):
    x = x_ref[...]
    ms = jnp.mean(x * x, axis=-1, keepdims=True)
    o_ref[...] = (x * lax.rsqrt(ms + EPS) * g_ref[...]).astype(o_ref.dtype)


def rms_rows(x, g, tm):
    t, d = x.shape
    tm = min(tm, t)
    return pl.pallas_call(
        _rms_rows_kernel, grid=(t // tm,),
        in_specs=[pl.BlockSpec((tm, d), lambda i: (i, 0)), pl.BlockSpec((1, d), lambda i: (0, 0))],
        out_specs=pl.BlockSpec((tm, d), lambda i: (i, 0)),
        out_shape=jax.ShapeDtypeStruct((t, d), BF16),
        compiler_params=_params(("parallel",)), name="rms_rows",
    )(x, g.reshape(1, d))


def _mm_kernel(a_ref, b_ref, *rest, mode):
    acc = jnp.dot(a_ref[...].astype(BF16), b_ref[...].astype(BF16), preferred_element_type=F32)
    if mode == "plain":
        (o_ref,) = rest
    elif mode == "tile_norm":
        g_ref, o_ref = rest
        ms = jnp.mean(acc * acc, axis=-1, keepdims=True)
        acc = acc * lax.rsqrt(ms + EPS) * g_ref[...]
    elif mode == "residual":
        x_ref, o_ref = rest
        acc = x_ref[...] + acc
    o_ref[...] = acc.astype(o_ref.dtype)


def matmul(a, b, *, tm, tn, out_dtype=F32, mode="plain", extra=None, layer=None, name="matmul"):
    m, k = a.shape
    n = b.shape[-1]
    tm, tn = min(tm, m), min(tn, n)
    assert m % tm == 0 and n % tn == 0, (m, n, tm, tn)
    if layer is None:
        b_spec = pl.BlockSpec((k, tn), lambda i, j: (0, j))
    else:
        b_spec = pl.BlockSpec((None, k, tn), lambda i, j: (layer, 0, j))
    in_specs = [pl.BlockSpec((tm, k), lambda i, j: (i, 0)), b_spec]
    args = [a, b]
    if mode == "tile_norm":
        in_specs.append(pl.BlockSpec((1, tn), lambda i, j: (0, 0)))
        args.append(extra.reshape(1, tn))
    elif mode == "residual":
        in_specs.append(pl.BlockSpec((tm, tn), lambda i, j: (i, j)))
        args.append(extra)
    return pl.pallas_call(
        functools.partial(_mm_kernel, mode=mode), grid=(m // tm, n // tn),
        in_specs=in_specs, out_specs=pl.BlockSpec((tm, tn), lambda i, j: (i, j)),
        out_shape=jax.ShapeDtypeStruct((m, n), out_dtype),
        compiler_params=_params(("parallel", "parallel")), name=name,
    )(*args)


def _in_proj_kernel(offs_ref, h_ref, wt_ref, wk_ref, wg_ref, o_ref, os_ref):
    del offs_ref
    h = h_ref[...]
    w = wt_ref[...].astype(BF16)
    o_ref[...] = lax.dot_general(h, w, NT_DIMS, preferred_element_type=F32)

    @pl.when(pl.program_id(1) == 0)
    def _():
        pad = LANES - wk_ref.shape[0] - wg_ref.shape[0]
        ws = jnp.concatenate([wk_ref[...], wg_ref[...], jnp.zeros((pad, h.shape[1]), F32)], axis=0)
        os_ref[...] = lax.dot_general(h, ws.astype(BF16), NT_DIMS, preferred_element_type=F32)


def in_proj(c, h, w_in_t, layer, row_offsets, *, tm, tn):
    t, d = h.shape
    tm = min(tm, t)
    nt = row_offsets.shape[0]
    small = lambda name: pl.BlockSpec((None, pl.Element(c.in_size[name]), pl.Element(d)),
                                      lambda i, j, offs: (layer, c.in_off[name], 0))
    grid_spec = pltpu.PrefetchScalarGridSpec(
        num_scalar_prefetch=1, grid=(t // tm, nt),
        in_specs=[pl.BlockSpec((tm, d), lambda i, j, offs: (i, 0)),
                  pl.BlockSpec((None, pl.Element(tn), pl.Element(d)),
                               lambda i, j, offs: (layer, offs[j] * OFFSET_UNIT, 0)),
                  small("kpe"), small("g_lr")],
        out_specs=[pl.BlockSpec((tm, tn), lambda i, j, offs: (i, j)),
                   pl.BlockSpec((tm, LANES), lambda i, j, offs: (i, 0))])
    return pl.pallas_call(
        _in_proj_kernel, grid_spec=grid_spec,
        out_shape=[jax.ShapeDtypeStruct((t, nt * tn), F32), jax.ShapeDtypeStruct((t, LANES), F32)],
        compiler_params=_params(("parallel", "arbitrary")), name="in_proj",
    )(row_offsets, h, w_in_t, w_in_t, w_in_t)


def _rope_slab(x, cos, sin_lo, sin_hi, half):
    return x * cos + pltpu.roll(x, LANES - half, 1) * sin_lo + pltpu.roll(x, half, 1) * sin_hi


def _q_prep_kernel(cq_ref, gl_ref, w_ref, gq_ref, cos_ref, slo_ref, shi_ref, o_ref, *, c):
    cq = cq_ref[...]
    ms = jnp.mean(cq * cq, axis=-1, keepdims=True)
    cqn = (cq * lax.rsqrt(ms + EPS) * gl_ref[...]).astype(BF16)
    q = jnp.dot(cqn, w_ref[...], preferred_element_type=F32)
    cos, slo, shi = cos_ref[...], slo_ref[...], shi_ref[...]
    g_n, g_p = gq_ref[:, :c.nope_hd], gq_ref[:, c.nope_hd:]
    for h in range(c.mla_heads):
        lo = h * c.hp
        qn = q[:, lo:lo + c.nope_hd]
        qp = q[:, lo + c.nope_hd:lo + c.hp]
        ss = jnp.sum(qn * qn, axis=-1, keepdims=True) + jnp.sum(qp * qp, axis=-1, keepdims=True)
        r = lax.rsqrt(ss / c.qk_hd + EPS)
        o_ref[:, lo:lo + c.nope_hd] = (qn * r * g_n).astype(o_ref.dtype)
        roped = _rope_slab(qp * r * g_p, cos, slo, shi, c.rope_hd // 2)
        o_ref[:, lo + c.nope_hd:lo + c.hp] = roped.astype(o_ref.dtype)


def q_prep(c, z, q_lora_g, w_uq_pad, qg_pad, tabs):
    t = z.shape[0]
    tm = min(c.tm_prep, t)
    n = c.mla_heads * c.hp
    row = lambda i: (i, 0)
    const = lambda i: (0, 0)
    cq_blk = _zblk(c, "cq", c.q_lora)
    return pl.pallas_call(
        functools.partial(_q_prep_kernel, c=c), grid=(t // tm,),
        in_specs=[pl.BlockSpec((tm, c.q_lora), lambda i: (i, cq_blk)), pl.BlockSpec((1, c.q_lora), const),
                  pl.BlockSpec((c.q_lora, n), const), pl.BlockSpec((1, c.hp), const),
                  pl.BlockSpec((tm, LANES), row), pl.BlockSpec((tm, LANES), row),
                  pl.BlockSpec((tm, LANES), row)],
        out_specs=pl.BlockSpec((tm, n), row),
        out_shape=jax.ShapeDtypeStruct((t, n), BF16),
        compiler_params=_params(("parallel",)), name="q_prep",
    )(z, q_lora_g.reshape(1, -1), w_uq_pad, qg_pad, *tabs)


def _kv_prep_kernel(ckv_ref, slab_ref, gl_ref, wuk_ref, wuv_ref, kg_ref, cos_ref, slo_ref, shi_ref,
                    ckvn_ref, kpe_ref, k_ref, v_ref, *, c):
    ckv = ckv_ref[...]
    ms = jnp.mean(ckv * ckv, axis=-1, keepdims=True)
    cn = ckv * lax.rsqrt(ms + EPS) * gl_ref[...]
    ckvn_ref[...] = cn
    slab = slab_ref[...]
    kpe_ref[...] = slab[:, :c.rope_hd]
    lane = lax.broadcasted_iota(jnp.int32, slab.shape, 1)
    kp = jnp.where(lane < c.rope_hd, slab, 0.0)
    cb = cn.astype(BF16)
    kn = jnp.dot(cb, wuk_ref[...].astype(BF16), preferred_element_type=F32)
    v_ref[...] = jnp.dot(cb, wuv_ref[...].astype(BF16), preferred_element_type=F32).astype(v_ref.dtype)
    sspe = jnp.sum(kp * kp, axis=-1, keepdims=True)
    g_n, g_p = kg_ref[:, :c.nope_hd], kg_ref[:, c.nope_hd:]
    rp = _rope_slab(kp * g_p, cos_ref[...], slo_ref[...], shi_ref[...], c.rope_hd // 2)
    for h in range(c.mla_heads):
        knh = kn[:, h * c.nope_hd:(h + 1) * c.nope_hd]
        ss = jnp.sum(knh * knh, axis=-1, keepdims=True) + sspe
        r = lax.rsqrt(ss / c.qk_hd + EPS)
        lo = h * c.hp
        k_ref[:, lo:lo + c.nope_hd] = (knh * r * g_n).astype(k_ref.dtype)
        k_ref[:, lo + c.nope_hd:lo + c.hp] = (rp * r).astype(k_ref.dtype)


def kv_prep(c, z, zs, kv_lora_g, w_uk, w_uv, layer, kg_pad, tabs):
    t = z.shape[0]
    tm = min(c.tm_prep, t)
    row = lambda i: (i, 0)
    const = lambda i: (0, 0)
    wmap = lambda i: (layer, 0, 0)
    ckv_blk = _zblk(c, "ckv", c.kv_lora)
    nk, nv = c.mla_heads * c.hp, c.mla_heads * c.v_hd
    wshape = (None,) + w_uk.shape[1:]
    return pl.pallas_call(
        functools.partial(_kv_prep_kernel, c=c), grid=(t // tm,),
        in_specs=[pl.BlockSpec((tm, c.kv_lora), lambda i: (i, ckv_blk)),
                  pl.BlockSpec((tm, LANES), row),
                  pl.BlockSpec((1, c.kv_lora), const),
                  pl.BlockSpec(wshape, wmap), pl.BlockSpec(wshape, wmap),
                  pl.BlockSpec((1, c.hp), const),
                  pl.BlockSpec((tm, LANES), row), pl.BlockSpec((tm, LANES), row),
                  pl.BlockSpec((tm, LANES), row)],
        out_specs=[pl.BlockSpec((tm, c.kv_lora), row), pl.BlockSpec((tm, c.rope_hd), row),
                   pl.BlockSpec((tm, nk), row), pl.BlockSpec((tm, nv), row)],
        out_shape=[jax.ShapeDtypeStruct((t, c.kv_lora), F32), jax.ShapeDtypeStruct((t, c.rope_hd), F32),
                   jax.ShapeDtypeStruct((t, nk), BF16), jax.ShapeDtypeStruct((t, nv), BF16)],
        compiler_params=_params(("parallel",)), name="kv_prep",
    )(z, zs, kv_lora_g.reshape(1, -1), w_uk, w_uv, kg_pad, *tabs)


def _flash_kernel(qi_ref, ki_ref, q_ref, k_ref, v_ref, g_ref, o_ref, m_scr, l_scr, acc_scr, *, scale, hps, hp,
                  v_hd):
    step = pl.program_id(2)
    qi, ki = qi_ref[step], ki_ref[step]

    @pl.when(ki == 0)
    def _():
        m_scr[...] = jnp.full(m_scr.shape, -jnp.inf, F32)
        l_scr[...] = jnp.zeros(l_scr.shape, F32)
        acc_scr[...] = jnp.zeros(acc_scr.shape, F32)

    def update(diagonal):
        for h in range(hps):
            qs, vs = slice(h * hp, (h + 1) * hp), slice(h * v_hd, (h + 1) * v_hd)
            s = lax.dot_general(q_ref[:, qs], k_ref[:, qs], NT_DIMS, preferred_element_type=F32) * scale
            if diagonal:
                row = lax.broadcasted_iota(jnp.int32, s.shape, 0)
                col = lax.broadcasted_iota(jnp.int32, s.shape, 1)
                s = jnp.where(col <= row, s, -jnp.inf)
            m_prev = m_scr[h]
            m_new = jnp.maximum(m_prev, jnp.max(s, axis=-1, keepdims=True))
            alpha = jnp.exp(m_prev - m_new)
            p = jnp.exp(s - m_new)
            l_scr[h] = alpha * l_scr[h] + jnp.sum(p, axis=-1, keepdims=True)
            acc_scr[h] = alpha * acc_scr[h] + jnp.dot(p.astype(BF16), v_ref[:, vs], preferred_element_type=F32)
            m_scr[h] = m_new

    @pl.when(ki < qi)
    def _():
        update(False)

    @pl.when(ki == qi)
    def _():
        update(True)
        for h in range(hps):
            vs = slice(h * v_hd, (h + 1) * v_hd)
            o_ref[:, vs] = (acc_scr[h] / l_scr[h] * _silu(g_ref[:, vs])).astype(o_ref.dtype)


def flash_prompt(c, q_full, k_full, v_full, z, batch, seq):
    t = q_full.shape[0]
    tq = min(c.tq_flash, seq)
    nq = seq // tq
    pairs = [(qi, ki) for qi in range(nq) for ki in range(qi + 1)]
    qi_tab = jnp.asarray([p[0] for p in pairs], jnp.int32)
    ki_tab = jnp.asarray([p[1] for p in pairs], jnp.int32)
    hps = c.flash_heads_per_step
    assert c.mla_heads % hps == 0
    ga_blk = _zblk(c, "g_a", hps * c.v_hd)
    qmap = lambda b, h, s, qt, kt: (b * nq + qt[s], h)
    kmap = lambda b, h, s, qt, kt: (b * nq + kt[s], h)
    grid_spec = pltpu.PrefetchScalarGridSpec(
        num_scalar_prefetch=2, grid=(batch, c.mla_heads // hps, len(pairs)),
        in_specs=[pl.BlockSpec((tq, hps * c.hp), qmap), pl.BlockSpec((tq, hps * c.hp), kmap),
                  pl.BlockSpec((tq, hps * c.v_hd), kmap),
                  pl.BlockSpec((tq, hps * c.v_hd), lambda b, h, s, qt, kt: (b * nq + qt[s], ga_blk + h))],
        out_specs=pl.BlockSpec((tq, hps * c.v_hd), qmap),
        scratch_shapes=[pltpu.VMEM((hps, tq, 1), F32), pltpu.VMEM((hps, tq, 1), F32),
                        pltpu.VMEM((hps, tq, c.v_hd), F32)])
    return pl.pallas_call(
        functools.partial(_flash_kernel, scale=c.qk_hd ** -0.5, hps=hps, hp=c.hp, v_hd=c.v_hd),
        grid_spec=grid_spec,
        out_shape=jax.ShapeDtypeStruct((t, c.branch_w), BF16),
        compiler_params=_params(("parallel", "parallel", "arbitrary")), name="flash_prompt",
    )(qi_tab, ki_tab, q_full, k_full, v_full, z)


def _score_block(c, nq, lhs_ref, cb, kpt, ctab, stab, qpa, qpb):
    hn = c.mla_heads * c.nope_hd
    sspe = jnp.sum(kpt * kpt, axis=0, keepdims=True)
    res = lax.dot_general(lhs_ref[...], cb, NT_DIMS, preferred_element_type=F32)
    s = (res[hn:, :] + jnp.dot(qpa, (kpt * ctab).astype(BF16), preferred_element_type=F32)
         + jnp.dot(qpb, (kpt * stab).astype(BF16), preferred_element_type=F32))
    out = []
    for h in range(c.mla_heads):
        kh = res[h * c.nope_hd:(h + 1) * c.nope_hd, :]
        ss = jnp.sum(kh * kh, axis=0, keepdims=True) + sspe
        out.append(s[h * nq:(h + 1) * nq, :] * lax.rsqrt(ss / c.qk_hd + EPS))
    return out


def _paged_kernel(pt_ref, wukt_ref, qlat_ref, qpa_ref, qpb_ref, cnew_ref, kpnew_ref, ctab_ref, stab_ref,
                  ctabn_ref, stabn_ref, ckv_hbm, kpt_hbm, o_ref, lhs_scr, cbuf, kbuf, csem, ksem, cbf_scr, s_scr,
                  m_scr, l_scr, acc_scr, *, c, layer, npages, nq):
    b, j = pl.program_id(0), pl.program_id(1)
    nb, steps = pl.num_programs(0), pl.num_programs(1)
    hn, hq = c.mla_heads * c.nope_hd, c.mla_heads * nq
    page = c.page_size
    t = b * steps + j
    slot = lax.rem(t, 2)

    def page_copies(bb, jj, sl, i):
        pg = pt_ref[bb, jj * npages + i]
        return (pltpu.make_async_copy(ckv_hbm.at[layer, pg], cbuf.at[sl, pl.ds(i * page, page), :], csem.at[sl]),
                pltpu.make_async_copy(kpt_hbm.at[layer, pg], kbuf.at[sl, i], ksem.at[sl]))

    @pl.when(t == 0)
    def _():
        lhs_scr[0:hn, :] = wukt_ref[...]
        for i in range(npages):
            for cp in page_copies(0, 0, 0, i):
                cp.start()

    @pl.when(j == 0)
    def _():
        lhs_scr[hn:hn + hq, :] = qlat_ref[...]
        m_scr[...] = jnp.full(m_scr.shape, -jnp.inf, F32)
        l_scr[...] = jnp.zeros(l_scr.shape, F32)
        acc_scr[...] = jnp.zeros(acc_scr.shape, F32)

    for i in range(npages):
        for cp in page_copies(b, j, slot, i):
            cp.wait()

    more = j + 1 < steps
    nxt_j = jnp.where(more, j + 1, 0)
    nxt_b = jnp.where(more, b, jnp.where(b + 1 < nb, b + 1, 0))
    qpa, qpb = qpa_ref[...], qpb_ref[...]

    def softmax_update(s, vals):
        m_prev = m_scr[...]
        m_new = jnp.maximum(m_prev, jnp.max(s, axis=-1, keepdims=True))
        alpha = jnp.exp(m_prev - m_new)
        p = jnp.exp(s - m_new)
        l_scr[...] = alpha * l_scr[...] + jnp.sum(p, axis=-1, keepdims=True)
        acc_scr[...] = alpha * acc_scr[...] + jnp.dot(p.astype(BF16), vals, preferred_element_type=F32)
        m_scr[...] = m_new

    for pp in range(npages // 2):
        rows = slice(pp * 2 * page, (pp + 1) * 2 * page)
        cb = cbuf[slot, rows, :].astype(BF16)
        cbf_scr[rows, :] = cb
        kpt = jnp.concatenate([kbuf[slot, 2 * pp], kbuf[slot, 2 * pp + 1]], axis=1)
        sc = _score_block(c, nq, lhs_scr, cb, kpt, ctab_ref[:, rows], stab_ref[:, rows], qpa, qpb)
        for h in range(c.mla_heads):
            s_scr[h * nq:(h + 1) * nq, rows] = sc[h]
        for i in (2 * pp, 2 * pp + 1):
            for cp in page_copies(nxt_b, nxt_j, 1 - slot, i):
                cp.start()
    softmax_update(s_scr[...], cbf_scr[...])

    @pl.when(t == nb * steps - 1)
    def _():
        for i in range(npages):
            for cp in page_copies(nxt_b, nxt_j, 1 - slot, i):
                cp.wait()

    @pl.when(j == steps - 1)
    def _():
        cb = jnp.concatenate([cnew_ref[...], jnp.zeros((page - nq, c.kv_lora), F32)], axis=0).astype(BF16)
        sc = _score_block(c, nq, lhs_scr, cb, kpnew_ref[...], ctabn_ref[...], stabn_ref[...], qpa, qpb)
        s = jnp.concatenate(sc, axis=0)
        col = lax.broadcasted_iota(jnp.int32, s.shape, 1)
        qidx = lax.broadcasted_iota(jnp.int32, s.shape, 0) % nq
        softmax_update(jnp.where(col <= qidx, s, -jnp.inf), cb)
        o_ref[...] = acc_scr[...] / l_scr[...]


def paged_sample_attend(c, layer, page_table, w_ukt, qlat, qpa, qpb, c_new, kpt_new, ctab_t, stab_t,
                        cache_ckv, cache_kpe_t):
    nb, n_pages = page_table.shape
    nq = c_new.shape[1]
    hq = c.mla_heads * nq
    hn = c.mla_heads * c.nope_hd
    npg = min(c.pages_per_step, n_pages)
    assert n_pages % npg == 0 and npg % 2 == 0
    steps = n_pages // npg
    rows = npg * c.page_size
    past = n_pages * c.page_size
    const2 = lambda b, j, pt: (0, 0)
    per_b = lambda b, j, pt: (b, 0, 0)
    tab_map = lambda b, j, pt: (0, j)
    in_specs = [pl.BlockSpec((hn, c.kv_lora), const2),
                pl.BlockSpec((None, hq, c.kv_lora), per_b),
                pl.BlockSpec((None, hq, c.rope_hd), per_b), pl.BlockSpec((None, hq, c.rope_hd), per_b),
                pl.BlockSpec((None, nq, c.kv_lora), per_b), pl.BlockSpec((None, c.rope_hd, c.page_size), per_b),
                pl.BlockSpec((c.rope_hd, rows), tab_map), pl.BlockSpec((c.rope_hd, rows), tab_map),
                pl.BlockSpec((c.rope_hd, c.page_size), const2), pl.BlockSpec((c.rope_hd, c.page_size), const2),
                pl.BlockSpec(memory_space=pl.ANY), pl.BlockSpec(memory_space=pl.ANY)]
    grid_spec = pltpu.PrefetchScalarGridSpec(
        num_scalar_prefetch=1, grid=(nb, steps), in_specs=in_specs,
        out_specs=pl.BlockSpec((None, hq, c.kv_lora), per_b),
        scratch_shapes=[pltpu.VMEM((hn + hq, c.kv_lora), BF16),
                        pltpu.VMEM((2, rows, c.kv_lora), F32), pltpu.VMEM((2, npg, c.rope_hd, c.page_size), F32),
                        pltpu.SemaphoreType.DMA((2,)), pltpu.SemaphoreType.DMA((2,)),
                        pltpu.VMEM((rows, c.kv_lora), BF16),
                        pltpu.VMEM((hq, rows), F32), pltpu.VMEM((hq, 1), F32), pltpu.VMEM((hq, 1), F32),
                        pltpu.VMEM((hq, c.kv_lora), F32)])
    return pl.pallas_call(
        functools.partial(_paged_kernel, c=c, layer=layer, npages=npg, nq=nq), grid_spec=grid_spec,
        out_shape=jax.ShapeDtypeStruct((nb, hq, c.kv_lora), F32),
        compiler_params=_params(("arbitrary", "arbitrary")), name="paged_sample_attend",
    )(page_table, w_ukt, qlat, qpa, qpb, c_new, kpt_new, ctab_t[:, :past], stab_t[:, :past],
      ctab_t[:, past:], stab_t[:, past:], cache_ckv, cache_kpe_t)


def _qlat_kernel(q_ref, w_ref, g_ref, o_ref, *, scale):
    q = (q_ref[...].astype(F32) * g_ref[...] * scale).astype(BF16)
    r = lax.dot_general(q, w_ref[...].astype(BF16), NT_DIMS, preferred_element_type=F32)
    o_ref[...] = r.reshape(o_ref.shape).astype(o_ref.dtype)


def absorbed_queries(c, q_full, row_blk, w_uk, layer, g_n, nb, nq):
    ts = nb * nq
    nope_blocks = c.hp // c.nope_hd
    return pl.pallas_call(
        functools.partial(_qlat_kernel, scale=c.qk_hd ** -0.5), grid=(c.mla_heads,),
        in_specs=[pl.BlockSpec((ts, c.nope_hd), lambda h: (row_blk, h * nope_blocks)),
                  pl.BlockSpec((None, c.kv_lora, c.nope_hd), lambda h: (layer, 0, h)),
                  pl.BlockSpec((1, c.nope_hd), lambda h: (0, 0))],
        out_specs=pl.BlockSpec((nb, None, nq, c.kv_lora), lambda h: (0, h, 0, 0)),
        out_shape=jax.ShapeDtypeStruct((nb, c.mla_heads, nq, c.kv_lora), BF16),
        compiler_params=_params(("parallel",)), name="absorbed_queries",
    )(q_full, w_uk, g_n)


def _latent_out_kernel(ol_ref, w_ref, g_ref, *rest):
    o_ref = rest[-1]
    ol = ol_ref[...]
    ol = ol.reshape(ol.shape[0] * ol.shape[1], ol.shape[2]).astype(BF16)
    o = jnp.dot(ol, w_ref[...].astype(BF16), preferred_element_type=F32)
    o_ref[...] = (o * _silu(g_ref[...])).astype(o_ref.dtype)


def latent_out(c, o_lat, w_uv, layer, z, row_blk, prev):
    nb, heads, nq, _ = o_lat.shape
    ts = nb * nq
    ga_blk = _zblk(c, "g_a", c.v_hd)
    in_specs = [pl.BlockSpec((nb, None, nq, c.kv_lora), lambda h: (0, h, 0, 0)),
                pl.BlockSpec((None, c.kv_lora, c.v_hd), lambda h: (layer, 0, h)),
                pl.BlockSpec((ts, c.v_hd), lambda h: (row_blk, ga_blk + h))]
    args = [o_lat, w_uv, z]
    aliases = _alias_prev(in_specs, args, prev, 0)
    return pl.pallas_call(
        _latent_out_kernel, grid=(heads,), in_specs=in_specs,
        out_specs=pl.BlockSpec((ts, c.v_hd), lambda h: (row_blk, h)),
        out_shape=jax.ShapeDtypeStruct(prev.shape, prev.dtype), input_output_aliases=aliases,
        compiler_params=_params(("parallel",)), name="latent_out",
    )(*args)


def _gla_kernel(q_ref, k_ref, v_ref, slab_ref, wg_ref, bg_ref, gb_ref, og_ref, *rest, c, chunk, has_s0,
                n_alias):
    rest = list(rest)
    s0_ref = rest.pop(0) if has_s0 else None
    o_ref, sout_ref, s_scr = rest[n_alias:]
    ci = pl.program_id(1)

    @pl.when(ci == 0)
    def _():
        if has_s0:
            s_scr[...] = s0_ref[...]
        else:
            s_scr[...] = jnp.zeros(s_scr.shape, F32)

    khd, vhd = c.gla_khd, c.gla_vhd
    x = jnp.dot(slab_ref[...].astype(BF16), wg_ref[...], preferred_element_type=F32) + bg_ref[...]
    logf = (jnp.minimum(x, 0.0) - jnp.log(1.0 + jnp.exp(-jnp.abs(x)))) / c.gate_tau
    row = lax.broadcasted_iota(jnp.int32, (chunk, chunk), 0)
    col = lax.broadcasted_iota(jnp.int32, (chunk, chunk), 1)
    tril = (col <= row).astype(F32)
    cum_all = jnp.dot(tril, logf, preferred_element_type=F32, precision=lax.Precision.HIGHEST)
    for h in range(c.gla_heads):
        ks, vs = slice(h * khd, (h + 1) * khd), slice(h * vhd, (h + 1) * vhd)
        cum = cum_all[:, ks]
        last = cum[chunk - 1:chunk, :]
        q = q_ref[:, ks] * khd ** -0.5
        k = k_ref[:, ks]
        v = v_ref[:, vs]
        s_prev = s_scr[h]
        inter = jnp.dot(q * jnp.exp(cum), s_prev, preferred_element_type=F32)
        a = jnp.zeros((chunk, chunk), F32)
        for s in range(chunk):
            t0 = (s // SUBLANES) * SUBLANES
            e = jnp.exp(cum[t0:, :] - cum[s:s + 1, :])
            colv = jnp.sum(q[t0:, :] * (k[s:s + 1, :] * e), axis=-1, keepdims=True)
            if t0:
                colv = jnp.concatenate([jnp.zeros((t0, 1), F32), colv], axis=0)
            a = jnp.where(col == s, colv, a)
        a = jnp.where(col <= row, a, 0.0)
        o = inter + jnp.dot(a, v, preferred_element_type=F32)
        ms = jnp.mean(o * o, axis=-1, keepdims=True)
        on = o * lax.rsqrt(ms + EPS) * og_ref[...]
        o_ref[:, vs] = (on * _silu(gb_ref[:, vs])).astype(o_ref.dtype)
        kt = k * jnp.exp(last - cum)
        upd = lax.dot_general(kt, v, TN_DIMS, preferred_element_type=F32)
        dcol = jnp.exp(jnp.transpose(jnp.broadcast_to(last, (LANES, khd))))
        for n in range(vhd // LANES):
            sl = slice(n * LANES, (n + 1) * LANES)
            s_scr[h, :, sl] = dcol * s_prev[:, sl] + upd[:, sl]

    @pl.when(ci == pl.num_programs(1) - 1)
    def _():
        sout_ref[...] = s_scr[...]


def gla(c, z, zs, w_gate_pad, b_gate, gla_o_g, s0, layer, depth, *, nb, length, row0, o_prev, s_prev):
    t = z.shape[0]
    chunk = math.gcd(length, c.gla_chunk)
    nch = length // chunk
    assert row0 % chunk == 0
    heads, khd, vhd = c.gla_heads, c.gla_khd, c.gla_vhd
    base = row0 // chunk
    kd, vd = heads * khd, heads * vhd
    rmap = lambda off: (lambda b, ci: (base + b * nch + ci, off))
    const = lambda b, ci: (0, 0)
    in_specs = [pl.BlockSpec((chunk, kd), rmap(_zblk(c, "gq", kd))),
                pl.BlockSpec((chunk, kd), rmap(_zblk(c, "gk", kd))),
                pl.BlockSpec((chunk, vd), rmap(_zblk(c, "gv", vd))),
                pl.BlockSpec((chunk, LANES), rmap(0)),
                pl.BlockSpec((LANES, kd), const), pl.BlockSpec((1, kd), const),
                pl.BlockSpec((chunk, vd), rmap(_zblk(c, "g_b", vd))),
                pl.BlockSpec((1, vhd), const)]
    args = [z, z, z, zs, w_gate_pad, b_gate.reshape(1, -1), z, gla_o_g.reshape(1, -1)]
    smap = lambda b, ci: (layer, b, 0, 0, 0)
    s_blk = (None, None, heads, khd, vhd)
    if s0 is not None:
        in_specs.append(pl.BlockSpec(s_blk, smap))
        args.append(s0)
    aliases = {}
    aliases.update(_alias_prev(in_specs, args, o_prev, 0))
    aliases.update(_alias_prev(in_specs, args, s_prev, 1))
    return pl.pallas_call(
        functools.partial(_gla_kernel, c=c, chunk=chunk, has_s0=s0 is not None, n_alias=len(aliases)),
        grid=(nb, nch), in_specs=in_specs,
        out_specs=[pl.BlockSpec((chunk, vd), rmap(0)), pl.BlockSpec(s_blk, smap)],
        out_shape=[jax.ShapeDtypeStruct((t, c.branch_w), BF16),
                   jax.ShapeDtypeStruct((depth, nb, heads, khd, vhd), F32)],
        scratch_shapes=[pltpu.VMEM((heads, khd, vhd), F32)], input_output_aliases=aliases,
        compiler_params=_params(("parallel", "arbitrary")), name="gla",
    )(*args)


def _mem_attn_kernel(q_ref, g_ref, k_ref, v_ref, gc_ref, *rest, heads, scale):
    o_ref = rest[-1]
    hd = q_ref.shape[1] // heads
    for h in range(heads):
        sl = slice(h * hd, (h + 1) * hd)
        q = q_ref[:, sl]
        ms = jnp.mean(q * q, axis=-1, keepdims=True)
        qn = (q * lax.rsqrt(ms + EPS) * g_ref[...]).astype(BF16)
        s = lax.dot_general(qn, k_ref[:, sl].astype(BF16), NT_DIMS, preferred_element_type=F32) * scale
        m = jnp.max(s, axis=-1, keepdims=True)
        p = jnp.exp(s - m)
        p = p / jnp.sum(p, axis=-1, keepdims=True)
        o = jnp.dot(p.astype(BF16), v_ref[:, sl].astype(BF16), preferred_element_type=F32)
        o_ref[:, sl] = (o * _silu(gc_ref[:, sl])).astype(o_ref.dtype)


def mem_attend(c, z, mem_q_g, mem_k, mem_v, layer, *, nb, length, row0, tq, prev):
    t = z.shape[0]
    tq = min(tq, length)
    nqt = length // tq
    assert row0 % tq == 0
    base = row0 // tq
    heads, hd, n_mem = c.mem_heads, c.mem_hd, mem_k.shape[2]
    w = heads * hd
    q_blk, g_blk = _zblk(c, "mq", w), _zblk(c, "g_c", w)
    kv_spec = pl.BlockSpec((None, None, n_mem, w), lambda b, i: (layer, b, 0, 0))
    in_specs = [pl.BlockSpec((tq, w), lambda b, i: (base + b * nqt + i, q_blk)),
                pl.BlockSpec((1, hd), lambda b, i: (0, 0)), kv_spec, kv_spec,
                pl.BlockSpec((tq, w), lambda b, i: (base + b * nqt + i, g_blk))]
    args = [z, mem_q_g.reshape(1, -1), mem_k, mem_v, z]
    aliases = _alias_prev(in_specs, args, prev, 0)
    return pl.pallas_call(
        functools.partial(_mem_attn_kernel, heads=heads, scale=hd ** -0.5), grid=(nb, nqt),
        in_specs=in_specs,
        out_specs=pl.BlockSpec((tq, w), lambda b, i: (base + b * nqt + i, 0)),
        out_shape=jax.ShapeDtypeStruct((t, c.branch_w), BF16), input_output_aliases=aliases,
        compiler_params=_params(("parallel", "arbitrary")), name="mem_attend",
    )(*args)


def _merge_kernel(ba_ref, bb_ref, bc_ref, w_ref, la_ref, lb_ref, lc_ref, bm_ref, o_ref):
    acc = None
    for n, (b_ref, l_ref) in enumerate(((ba_ref, la_ref), (bb_ref, lb_ref), (bc_ref, lc_ref))):
        proj = jnp.dot(b_ref[...], w_ref[n].astype(BF16), preferred_element_type=F32)
        term = _sigmoid(l_ref[...] + bm_ref[n:n + 1, :]) * proj
        acc = term if acc is None else acc + term
    o_ref[...] = acc.astype(o_ref.dtype)


def merge(c, br_a, br_b, br_c, w_branch, layer, z, b_merge):
    t = br_a.shape[0]
    d = c.d_model
    tm, tn = min(c.tm_merge, t), min(c.tn_merge, d)
    nj = d // tn
    l_blk = _zblk(c, "m_logit", tn)
    bmap = lambda j, i: (i, 0)
    lmap = lambda n: (lambda j, i: (i, l_blk + n * nj + j))
    return pl.pallas_call(
        _merge_kernel, grid=(nj, t // tm),
        in_specs=[pl.BlockSpec((tm, c.branch_w), bmap)] * 3
        + [pl.BlockSpec((None, c.n_branch, c.branch_w, tn), lambda j, i: (layer, 0, 0, j))]
        + [pl.BlockSpec((tm, tn), lmap(n)) for n in range(3)]
        + [pl.BlockSpec((c.n_branch, tn), lambda j, i: (0, j))],
        out_specs=pl.BlockSpec((tm, tn), lambda j, i: (i, j)),
        out_shape=jax.ShapeDtypeStruct((t, d), BF16),
        compiler_params=_params(("parallel", "parallel")), name="merge",
    )(br_a, br_b, br_c, w_branch, z, z, z, b_merge)


def _rope_tables(c, pos):
    half = c.rope_hd // 2
    inv = jnp.power(c.rope_theta, -jnp.arange(half, dtype=F32) / half)
    ang = pos.astype(F32)[:, None] * inv[None, :]
    cos, sin = jnp.cos(ang), jnp.sin(ang)
    z = jnp.zeros((pos.shape[0], LANES - c.rope_hd), F32)
    zh = jnp.zeros_like(cos)
    slab_tabs = (jnp.concatenate([cos, cos, z], 1), jnp.concatenate([-sin, zh, z], 1),
                 jnp.concatenate([zh, sin, z], 1))
    return slab_tabs, jnp.concatenate([cos, cos], 1).T, jnp.concatenate([sin, -sin], 1).T


def _in_proj_offsets(c):
    offs = []
    for n in Z_NAMES:
        offs += [c.in_off[n] + k * c.tn_in for k in range(c.in_size[n] // c.tn_in)]
    assert all(o % OFFSET_UNIT == 0 for o in offs)
    return jnp.asarray([o // OFFSET_UNIT for o in offs], jnp.int32)


def _forward(c, x_prompt, x_sample, mem_prompt, cache_ckv, cache_kpe, cache_mem_k, cache_mem_v, state_gla,
             page_table, **w):
    c = _derive(c)
    batch, seq, d = x_prompt.shape
    nb, nq = x_sample.shape[:2]
    n_mem = mem_prompt.shape[1]
    depth = w["w_in"].shape[0]
    tp, ts = batch * seq, nb * nq
    t = tp + ts
    past = page_table.shape[1] * c.page_size
    assert tp % ts == 0
    heads = c.mla_heads

    pos_all = jnp.concatenate([jnp.tile(jnp.arange(seq, dtype=jnp.int32), batch),
                               jnp.tile(past + jnp.arange(nq, dtype=jnp.int32), nb)])
    tabs, _, _ = _rope_tables(c, pos_all)
    _, ctab_t, stab_t = _rope_tables(c, jnp.arange(past + c.page_size, dtype=jnp.int32))

    x_all = jnp.concatenate([x_prompt.reshape(tp, d), x_sample.reshape(ts, d)], axis=0)
    mem_flat = mem_prompt.reshape(batch * n_mem, d)
    w_in_t = jnp.swapaxes(w["w_in"], 1, 2)
    cache_kpe_t = jnp.swapaxes(cache_kpe, 2, 3)
    mem_k_all = cache_mem_k.reshape(depth, nb, n_mem, -1)
    mem_v_all = cache_mem_v.reshape(depth, nb, n_mem, -1)
    in_offs = _in_proj_offsets(c)
    half = c.rope_hd // 2
    outs = {k: [] for k in ("ckv_p", "kpe_p", "gla_p", "mk_p", "mv_p", "ckv_s", "kpe_s")}
    gla_s = None

    for l in range(depth):
        wq = w["w_uq"][l].reshape(c.q_lora, heads, c.qk_hd)
        w_uq_pad = jnp.pad(wq, ((0, 0), (0, 0), (0, c.hp - c.qk_hd))).reshape(c.q_lora, heads * c.hp)
        w_uq_pad = w_uq_pad.astype(BF16)
        padg = lambda g: jnp.pad(g, (0, c.hp - c.qk_hd)).reshape(1, c.hp)
        qg_pad, kg_pad = padg(w["mla_q_g"][l]), padg(w["mla_k_g"][l])
        w_ukt = w["w_uk"][l].T.astype(BF16)
        w_gate = jnp.zeros((LANES, c.gla_kd), F32).at[c.rope_hd:c.rope_hd + c.gate_rank].set(w["w_gate2"][l])
        w_gate = w_gate.astype(BF16)

        h = rms_rows(x_all, w["norm_w"][l], c.tm_rows)
        z, zs = in_proj(c, h, w_in_t, l, in_offs, tm=c.tm_mm, tn=c.tn_in)
        m_n = rms_rows(mem_flat, w["mem_norm_w"][l], c.tm_rows)
        mk = matmul(m_n, w["w_mk"], tm=c.tm_mm, tn=c.mem_hd, mode="tile_norm", extra=w["mem_k_g"][l],
                    layer=l, name="mem_k")
        mv = matmul(m_n, w["w_mv"], tm=c.tm_mm, tn=c.tn_in, layer=l, name="mem_v")
        q_full = q_prep(c, z, w["q_lora_g"][l], w_uq_pad, qg_pad, tabs)
        ckv_n, kpe, k_full, v_full = kv_prep(c, z, zs, w["kv_lora_g"][l], w["w_uk"], w["w_uv"], l, kg_pad, tabs)
        br_a = flash_prompt(c, q_full, k_full, v_full, z, batch, seq)
        g_n = w["mla_k_g"][l][:c.nope_hd].reshape(1, -1)
        g_p = w["mla_k_g"][l][c.nope_hd:]
        qlat = absorbed_queries(c, q_full, tp // ts, w["w_uk"], l, g_n, nb, nq)
        qlat = qlat.reshape(nb, heads * nq, c.kv_lora)
        q_pe = q_full[tp:].reshape(nb, nq, heads, c.hp)[..., c.nope_hd:c.qk_hd].astype(F32)
        q_pe = jnp.swapaxes(q_pe, 1, 2).reshape(nb, heads * nq, c.rope_hd) * c.qk_hd ** -0.5
        qpa = (q_pe * g_p).astype(BF16)
        qpb = (jnp.concatenate([q_pe[..., half:], q_pe[..., :half]], -1) * g_p).astype(BF16)
        c_new = ckv_n[tp:].reshape(nb, nq, c.kv_lora)
        kp_new = kpe[tp:].reshape(nb, nq, c.rope_hd)
        kpt_new = jnp.pad(jnp.swapaxes(kp_new, 1, 2), ((0, 0), (0, 0), (0, c.page_size - nq)))
        o_lat = paged_sample_attend(c, l, page_table, w_ukt, qlat, qpa, qpb, c_new, kpt_new, ctab_t, stab_t,
                                    cache_ckv, cache_kpe_t)
        br_a = latent_out(c, o_lat.reshape(nb, heads, nq, c.kv_lora), w["w_uv"], l, z, tp // ts, br_a)
        br_b, gla_p = gla(c, z, zs, w_gate, w["b_gate"][l], w["gla_o_g"][l], None, 0, 1,
                          nb=batch, length=seq, row0=0, o_prev=None, s_prev=None)
        br_b, gla_s = gla(c, z, zs, w_gate, w["b_gate"][l], w["gla_o_g"][l], state_gla, l, depth,
                          nb=nb, length=nq, row0=tp, o_prev=br_b, s_prev=gla_s)
        mem4 = lambda a: a.reshape(1, batch, n_mem, c.branch_w)
        br_c = mem_attend(c, z, w["mem_q_g"][l], mem4(mk), mem4(mv), 0, nb=batch, length=seq, row0=0,
                          tq=c.tq_mem, prev=None)
        br_c = mem_attend(c, z, w["mem_q_g"][l], mem_k_all, mem_v_all, l, nb=nb, length=nq, row0=tp, tq=nq,
                          prev=br_c)
        mix = merge(c, br_a, br_b, br_c, w["w_branch"], l, z, w["b_merge"][l])
        x_all = matmul(mix, w["w_out"], tm=c.tm_mm, tn=c.tn_in, mode="residual", extra=x_all, layer=l,
                       name="out_proj")

        outs["ckv_p"].append(ckv_n[:tp].reshape(batch, seq, -1))
        outs["kpe_p"].append(kpe[:tp].reshape(batch, seq, -1))
        outs["gla_p"].append(gla_p[0])
        outs["mk_p"].append(mk.reshape(batch, n_mem, c.mem_heads, c.mem_hd))
        outs["mv_p"].append(mv.reshape(batch, n_mem, c.mem_heads, c.mem_hd))
        outs["ckv_s"].append(c_new)
        outs["kpe_s"].append(kp_new)

    st = lambda k: jnp.stack(outs[k])
    return (x_all[:tp].reshape(batch, seq, d), x_all[tp:].reshape(nb, nq, d),
            st("ckv_p"), st("kpe_p"), st("gla_p"), st("mk_p"), st("mv_p"), st("ckv_s"), st("kpe_s"), gla_s)


def kernel(x_prompt, x_sample, mem_prompt, cache_ckv, cache_kpe, cache_mem_k, cache_mem_v, state_gla, page_table, norm_w, w_in, q_lora_g, kv_lora_g, w_uq, mla_q_g, mla_k_g, w_uk, w_uv, w_gate2, b_gate, gla_o_g, mem_norm_w, w_mk, w_mv, mem_q_g, mem_k_g, w_branch, b_merge, w_out):
    return _forward(default_config(), x_prompt, x_sample, mem_prompt, cache_ckv, cache_kpe, cache_mem_k,
                    cache_mem_v, state_gla, page_table,
                    norm_w=norm_w, w_in=w_in, q_lora_g=q_lora_g, kv_lora_g=kv_lora_g, w_uq=w_uq,
                    mla_q_g=mla_q_g, mla_k_g=mla_k_g, w_uk=w_uk, w_uv=w_uv, w_gate2=w_gate2, b_gate=b_gate,
                    gla_o_g=gla_o_g, mem_norm_w=mem_norm_w, w_mk=w_mk, w_mv=w_mv, mem_q_g=mem_q_g,
                    mem_k_g=mem_k_g, w_branch=w_branch, b_merge=b_merge, w_out=w_out)
```

```python
import functools
import math
from types import SimpleNamespace

import jax
import jax.numpy as jnp
import numpy as np
from jax import lax
from jax.experimental import pallas as pl
from jax.experimental.pallas import tpu as pltpu

F32 = jnp.float32
BF16 = jnp.bfloat16
EPS = 1e-6
LANES = 128
SUBLANES = 8
VMEM_LIMIT = 56 * 1024 * 1024

OFFSET_UNIT = 16

NT_DIMS = (((1,), (1,)), ((), ()))
TN_DIMS = (((0,), (0,)), ((), ()))

IN_NAMES = ("cq", "ckv", "kpe", "g_a", "gq", "gk", "gv", "g_lr", "g_b", "mq", "g_c", "m_logit")
Z_NAMES = ("m_logit", "g_a", "g_b", "mq", "g_c", "gv", "gq", "gk", "cq", "ckv")


def default_config():
    d_model = 4096
    branch_w = d_model // 2
    return SimpleNamespace(
        d_model=d_model, branch_w=branch_w, n_branch=3,
        v_hd=128, nope_hd=128, rope_hd=64, q_lora=d_model // 4, kv_lora=512,
        rope_theta=10000.0,
        gla_heads=4, gla_kd=d_model // 4, gla_vd=branch_w, gate_rank=16, gate_tau=16.0,
        gla_chunk=64,
        mem_heads=4, page_size=128,
        tm_mm=1024, tn_mm=1024, tn_in=512, tm_rows=512, tm_prep=256, tq_flash=512, tq_mem=512,
        pages_per_step=16, softmax_pairs=8, tm_merge=512, tn_merge=512, flash_heads_per_step=2,
    )


def _derive(c):
    c.mla_heads = c.branch_w // c.v_hd
    c.qk_hd = c.nope_hd + c.rope_hd
    c.hp = c.nope_hd + LANES
    c.gla_khd = c.gla_kd // c.gla_heads
    c.gla_vhd = c.gla_vd // c.gla_heads
    c.mem_hd = c.branch_w // c.mem_heads
    sizes = (c.q_lora, c.kv_lora, c.rope_hd, c.branch_w, c.gla_kd, c.gla_kd, c.gla_vd,
             c.gate_rank, c.branch_w, c.branch_w, c.branch_w, c.n_branch * c.d_model)
    c.in_size = dict(zip(IN_NAMES, sizes))
    c.in_off = dict(zip(IN_NAMES, np.concatenate([[0], np.cumsum(sizes)[:-1]]).astype(int).tolist()))
    zo, acc = {}, 0
    for n in Z_NAMES:
        zo[n] = acc
        acc += c.in_size[n]
    c.z_off, c.z_cols = zo, acc
    assert c.rope_hd + c.gate_rank <= LANES and c.nope_hd % LANES == 0
    assert all(c.in_size[n] % c.tn_in == 0 for n in Z_NAMES)
    return c


def _zblk(c, name, width):
    off = c.z_off[name]
    assert off % width == 0, (name, off, width)
    return off // width


def _params(sem):
    return pltpu.CompilerParams(dimension_semantics=sem, vmem_limit_bytes=VMEM_LIMIT)


def _silu(g):
    return g / (1.0 + jnp.exp(-g))


def _sigmoid(x):
    return 1.0 / (1.0 + jnp.exp(-x))


def _alias_prev(in_specs, args, prev, out_index):
    if prev is None:
        return {}
    in_specs.append(pl.BlockSpec(memory_space=pl.ANY))
    args.append(prev)
    return {len(args) - 1: out_index}


def _rms_rows_kernel(x_ref, g_ref, o_ref):
    x = x_ref[...]
    ms = jnp.mean(x * x, axis=-1, keepdims=True)
    o_ref[...] = (x * lax.rsqrt(ms + EPS) * g_ref[...]).astype(o_ref.dtype)


def rms_rows(x, g, tm):
    t, d = x.shape
    tm = min(tm, t)
    return pl.pallas_call(
        _rms_rows_kernel, grid=(t // tm,),
        in_specs=[pl.BlockSpec((tm, d), lambda i: (i, 0)), pl.BlockSpec((1, d), lambda i: (0, 0))],
        out_specs=pl.BlockSpec((tm, d), lambda i: (i, 0)),
        out_shape=jax.ShapeDtypeStruct((t, d), BF16),
        compiler_params=_params(("parallel",)), name="rms_rows",
    )(x, g.reshape(1, d))


def _mm_kernel(a_ref, b_ref, *rest, mode):
    acc = jnp.dot(a_ref[...].astype(BF16), b_ref[...].astype(BF16), preferred_element_type=F32)
    if mode == "plain":
        (o_ref,) = rest
    elif mode == "tile_norm":
        g_ref, o_ref = rest
        ms = jnp.mean(acc * acc, axis=-1, keepdims=True)
        acc = acc * lax.rsqrt(ms + EPS) * g_ref[...]
    elif mode == "residual":
        x_ref, o_ref = rest
        acc = x_ref[...] + acc
    o_ref[...] = acc.astype(o_ref.dtype)


def matmul(a, b, *, tm, tn, out_dtype=F32, mode="plain", extra=None, layer=None, name="matmul"):
    m, k = a.shape
    n = b.shape[-1]
    tm, tn = min(tm, m), min(tn, n)
    assert m % tm == 0 and n % tn == 0, (m, n, tm, tn)
    if layer is None:
        b_spec = pl.BlockSpec((k, tn), lambda i, j: (0, j))
    else:
        b_spec = pl.BlockSpec((None, k, tn), lambda i, j: (layer, 0, j))
    in_specs = [pl.BlockSpec((tm, k), lambda i, j: (i, 0)), b_spec]
    args = [a, b]
    if mode == "tile_norm":
        in_specs.append(pl.BlockSpec((1, tn), lambda i, j: (0, 0)))
        args.append(extra.reshape(1, tn))
    elif mode == "residual":
        in_specs.append(pl.BlockSpec((tm, tn), lambda i, j: (i, j)))
        args.append(extra)
    return pl.pallas_call(
        functools.partial(_mm_kernel, mode=mode), grid=(m // tm, n // tn),
        in_specs=in_specs, out_specs=pl.BlockSpec((tm, tn), lambda i, j: (i, j)),
        out_shape=jax.ShapeDtypeStruct((m, n), out_dtype),
        compiler_params=_params(("parallel", "parallel")), name=name,
    )(*args)


def _in_proj_kernel(offs_ref, h_ref, wt_ref, wk_ref, wg_ref, o_ref, os_ref):
    del offs_ref
    h = h_ref[...]
    w = wt_ref[...].astype(BF16)
    o_ref[...] = lax.dot_general(h, w, NT_DIMS, preferred_element_type=F32)

    @pl.when(pl.program_id(1) == 0)
    def _():
        pad = LANES - wk_ref.shape[0] - wg_ref.shape[0]
        ws = jnp.concatenate([wk_ref[...], wg_ref[...], jnp.zeros((pad, h.shape[1]), F32)], axis=0)
        os_ref[...] = lax.dot_general(h, ws.astype(BF16), NT_DIMS, preferred_element_type=F32)


def in_proj(c, h, w_in_t, layer, row_offsets, *, tm, tn):
    t, d = h.shape
    tm = min(tm, t)
    nt = row_offsets.shape[0]
    small = lambda name: pl.BlockSpec((None, pl.Element(c.in_size[name]), pl.Element(d)),
                                      lambda i, j, offs: (layer, c.in_off[name], 0))
    grid_spec = pltpu.PrefetchScalarGridSpec(
        num_scalar_prefetch=1, grid=(t // tm, nt),
        in_specs=[pl.BlockSpec((tm, d), lambda i, j, offs: (i, 0)),
                  pl.BlockSpec((None, pl.Element(tn), pl.Element(d)),
                               lambda i, j, offs: (layer, offs[j] * OFFSET_UNIT, 0)),
                  small("kpe"), small("g_lr")],
        out_specs=[pl.BlockSpec((tm, tn), lambda i, j, offs: (i, j)),
                   pl.BlockSpec((tm, LANES), lambda i, j, offs: (i, 0))])
    return pl.pallas_call(
        _in_proj_kernel, grid_spec=grid_spec,
        out_shape=[jax.ShapeDtypeStruct((t, nt * tn), F32), jax.ShapeDtypeStruct((t, LANES), F32)],
        compiler_params=_params(("parallel", "arbitrary")), name="in_proj",
    )(row_offsets, h, w_in_t, w_in_t, w_in_t)


def _rope_slab(x, cos, sin_lo, sin_hi, half):
    return x * cos + pltpu.roll(x, LANES - half, 1) * sin_lo + pltpu.roll(x, half, 1) * sin_hi


def _q_prep_kernel(cq_ref, gl_ref, w_ref, gq_ref, cos_ref, slo_ref, shi_ref, o_ref, *, c):
    cq = cq_ref[...]
    ms = jnp.mean(cq * cq, axis=-1, keepdims=True)
    cqn = (cq * lax.rsqrt(ms + EPS) * gl_ref[...]).astype(BF16)
    q = jnp.dot(cqn, w_ref[...], preferred_element_type=F32)
    cos, slo, shi = cos_ref[...], slo_ref[...], shi_ref[...]
    g_n, g_p = gq_ref[:, :c.nope_hd], gq_ref[:, c.nope_hd:]
    for h in range(c.mla_heads):
        lo = h * c.hp
        qn = q[:, lo:lo + c.nope_hd]
        qp = q[:, lo + c.nope_hd:lo + c.hp]
        ss = jnp.sum(qn * qn, axis=-1, keepdims=True) + jnp.sum(qp * qp, axis=-1, keepdims=True)
        r = lax.rsqrt(ss / c.qk_hd + EPS)
        o_ref[:, lo:lo + c.nope_hd] = (qn * r * g_n).astype(o_ref.dtype)
        roped = _rope_slab(qp * r * g_p, cos, slo, shi, c.rope_hd // 2)
        o_ref[:, lo + c.nope_hd:lo + c.hp] = roped.astype(o_ref.dtype)


def q_prep(c, z, q_lora_g, w_uq_pad, qg_pad, tabs):
    t = z.shape[0]
    tm = min(c.tm_prep, t)
    n = c.mla_heads * c.hp
    row = lambda i: (i, 0)
    const = lambda i: (0, 0)
    cq_blk = _zblk(c, "cq", c.q_lora)
    return pl.pallas_call(
        functools.partial(_q_prep_kernel, c=c), grid=(t // tm,),
        in_specs=[pl.BlockSpec((tm, c.q_lora), lambda i: (i, cq_blk)), pl.BlockSpec((1, c.q_lora), const),
                  pl.BlockSpec((c.q_lora, n), const), pl.BlockSpec((1, c.hp), const),
                  pl.BlockSpec((tm, LANES), row), pl.BlockSpec((tm, LANES), row),
                  pl.BlockSpec((tm, LANES), row)],
        out_specs=pl.BlockSpec((tm, n), row),
        out_shape=jax.ShapeDtypeStruct((t, n), BF16),
        compiler_params=_params(("parallel",)), name="q_prep",
    )(z, q_lora_g.reshape(1, -1), w_uq_pad, qg_pad, *tabs)


def _kv_prep_kernel(ckv_ref, slab_ref, gl_ref, wuk_ref, wuv_ref, kg_ref, cos_ref, slo_ref, shi_ref,
                    ckvn_ref, kpe_ref, k_ref, v_ref, *, c):
    ckv = ckv_ref[...]
    ms = jnp.mean(ckv * ckv, axis=-1, keepdims=True)
    cn = ckv * lax.rsqrt(ms + EPS) * gl_ref[...]
    ckvn_ref[...] = cn
    slab = slab_ref[...]
    kpe_ref[...] = slab[:, :c.rope_hd]
    lane = lax.broadcasted_iota(jnp.int32, slab.shape, 1)
    kp = jnp.where(lane < c.rope_hd, slab, 0.0)
    cb = cn.astype(BF16)
    kn = jnp.dot(cb, wuk_ref[...].astype(BF16), preferred_element_type=F32)
    v_ref[...] = jnp.dot(cb, wuv_ref[...].astype(BF16), preferred_element_type=F32).astype(v_ref.dtype)
    sspe = jnp.sum(kp * kp, axis=-1, keepdims=True)
    g_n, g_p = kg_ref[:, :c.nope_hd], kg_ref[:, c.nope_hd:]
    rp = _rope_slab(kp * g_p, cos_ref[...], slo_ref[...], shi_ref[...], c.rope_hd // 2)
    for h in range(c.mla_heads):
        knh = kn[:, h * c.nope_hd:(h + 1) * c.nope_hd]
        ss = jnp.sum(knh * knh, axis=-1, keepdims=True) + sspe
        r = lax.rsqrt(ss / c.qk_hd + EPS)
        lo = h * c.hp
        r = r * c.qk_hd ** -0.5
        k_ref[:, lo:lo + c.nope_hd] = (knh * r * g_n).astype(k_ref.dtype)
        k_ref[:, lo + c.nope_hd:lo + c.hp] = (rp * r).astype(k_ref.dtype)


def kv_prep(c, z, zs, kv_lora_g, w_uk, w_uv, layer, kg_pad, tabs):
    t = z.shape[0]
    tm = min(c.tm_prep, t)
    row = lambda i: (i, 0)
    const = lambda i: (0, 0)
    wmap = lambda i: (layer, 0, 0)
    ckv_blk = _zblk(c, "ckv", c.kv_lora)
    nk, nv = c.mla_heads * c.hp, c.mla_heads * c.v_hd
    wshape = (None,) + w_uk.shape[1:]
    return pl.pallas_call(
        functools.partial(_kv_prep_kernel, c=c), grid=(t // tm,),
        in_specs=[pl.BlockSpec((tm, c.kv_lora), lambda i: (i, ckv_blk)),
                  pl.BlockSpec((tm, LANES), row),
                  pl.BlockSpec((1, c.kv_lora), const),
                  pl.BlockSpec(wshape, wmap), pl.BlockSpec(wshape, wmap),
                  pl.BlockSpec((1, c.hp), const),
                  pl.BlockSpec((tm, LANES), row), pl.BlockSpec((tm, LANES), row),
                  pl.BlockSpec((tm, LANES), row)],
        out_specs=[pl.BlockSpec((tm, c.kv_lora), row), pl.BlockSpec((tm, c.rope_hd), row),
                   pl.BlockSpec((tm, nk), row), pl.BlockSpec((tm, nv), row)],
        out_shape=[jax.ShapeDtypeStruct((t, c.kv_lora), F32), jax.ShapeDtypeStruct((t, c.rope_hd), F32),
                   jax.ShapeDtypeStruct((t, nk), BF16), jax.ShapeDtypeStruct((t, nv), BF16)],
        compiler_params=_params(("parallel",)), name="kv_prep",
    )(z, zs, kv_lora_g.reshape(1, -1), w_uk, w_uv, kg_pad, *tabs)


def _flash_kernel(qi_ref, ki_ref, q_ref, k_ref, v_ref, g_ref, o_ref, m_scr, l_scr, acc_scr, *, hps, hp, v_hd):
    step = pl.program_id(2)
    qi, ki = qi_ref[step], ki_ref[step]

    @pl.when(ki == 0)
    def _():
        m_scr[...] = jnp.full(m_scr.shape, -jnp.inf, F32)
        l_scr[...] = jnp.zeros(l_scr.shape, F32)
        acc_scr[...] = jnp.zeros(acc_scr.shape, F32)

    def update(diagonal):
        for h in range(hps):
            qs, vs = slice(h * hp, (h + 1) * hp), slice(h * v_hd, (h + 1) * v_hd)
            s = lax.dot_general(q_ref[:, qs], k_ref[:, qs], NT_DIMS, preferred_element_type=F32)
            if diagonal:
                row = lax.broadcasted_iota(jnp.int32, s.shape, 0)
                col = lax.broadcasted_iota(jnp.int32, s.shape, 1)
                s = jnp.where(col <= row, s, -jnp.inf)
            m_prev = m_scr[h]
            m_new = jnp.maximum(m_prev, jnp.max(s, axis=-1, keepdims=True))
            alpha = jnp.exp(m_prev - m_new)
            p = jnp.exp(s - m_new)
            l_scr[h] = alpha * l_scr[h] + jnp.sum(p, axis=-1, keepdims=True)
            acc_scr[h] = alpha * acc_scr[h] + jnp.dot(p.astype(BF16), v_ref[:, vs], preferred_element_type=F32)
            m_scr[h] = m_new

    @pl.when(ki < qi)
    def _():
        update(False)

    @pl.when(ki == qi)
    def _():
        update(True)
        for h in range(hps):
            vs = slice(h * v_hd, (h + 1) * v_hd)
            o_ref[:, vs] = (acc_scr[h] / l_scr[h] * _silu(g_ref[:, vs])).astype(o_ref.dtype)


def flash_prompt(c, q_full, k_full, v_full, z, batch, seq):
    t = q_full.shape[0]
    tq = min(c.tq_flash, seq)
    nq = seq // tq
    pairs = [(qi, ki) for qi in range(nq) for ki in range(qi + 1)]
    qi_tab = jnp.asarray([p[0] for p in pairs], jnp.int32)
    ki_tab = jnp.asarray([p[1] for p in pairs], jnp.int32)
    hps = c.flash_heads_per_step
    assert c.mla_heads % hps == 0
    ga_blk = _zblk(c, "g_a", hps * c.v_hd)
    qmap = lambda b, h, s, qt, kt: (b * nq + qt[s], h)
    kmap = lambda b, h, s, qt, kt: (b * nq + kt[s], h)
    grid_spec = pltpu.PrefetchScalarGridSpec(
        num_scalar_prefetch=2, grid=(batch, c.mla_heads // hps, len(pairs)),
        in_specs=[pl.BlockSpec((tq, hps * c.hp), qmap), pl.BlockSpec((tq, hps * c.hp), kmap),
                  pl.BlockSpec((tq, hps * c.v_hd), kmap),
                  pl.BlockSpec((tq, hps * c.v_hd), lambda b, h, s, qt, kt: (b * nq + qt[s], ga_blk + h))],
        out_specs=pl.BlockSpec((tq, hps * c.v_hd), qmap),
        scratch_shapes=[pltpu.VMEM((hps, tq, 1), F32), pltpu.VMEM((hps, tq, 1), F32),
                        pltpu.VMEM((hps, tq, c.v_hd), F32)])
    return pl.pallas_call(
        functools.partial(_flash_kernel, hps=hps, hp=c.hp, v_hd=c.v_hd),
        grid_spec=grid_spec,
        out_shape=jax.ShapeDtypeStruct((t, c.branch_w), BF16),
        compiler_params=_params(("parallel", "parallel", "arbitrary")), name="flash_prompt",
    )(qi_tab, ki_tab, q_full, k_full, v_full, z)


def _score_block(c, nq, lhs_ref, cb, kpt, ctab, stab, qpa, qpb):
    hn = c.mla_heads * c.nope_hd
    sspe = jnp.sum(kpt * kpt, axis=0, keepdims=True)
    res = lax.dot_general(lhs_ref[...], cb, NT_DIMS, preferred_element_type=F32)
    s = (res[hn:, :] + jnp.dot(qpa, (kpt * ctab).astype(BF16), preferred_element_type=F32)
         + jnp.dot(qpb, (kpt * stab).astype(BF16), preferred_element_type=F32))
    out = []
    for h in range(c.mla_heads):
        kh = res[h * c.nope_hd:(h + 1) * c.nope_hd, :]
        ss = jnp.sum(kh * kh, axis=0, keepdims=True) + sspe
        out.append(s[h * nq:(h + 1) * nq, :] * lax.rsqrt(ss / c.qk_hd + EPS))
    return out


def _paged_kernel(pt_ref, wukt_ref, qlat_ref, qpa_ref, qpb_ref, cnew_ref, kpnew_ref, ctab_ref, stab_ref,
                  ctabn_ref, stabn_ref, ckv_hbm, kpt_hbm, o_ref, lhs_scr, cbuf, kbuf, csem, ksem, cbf_scr, s_scr,
                  m_scr, l_scr, acc_scr, *, c, layer, npages, nq):
    b, j = pl.program_id(0), pl.program_id(1)
    nb, steps = pl.num_programs(0), pl.num_programs(1)
    hn, hq = c.mla_heads * c.nope_hd, c.mla_heads * nq
    page = c.page_size
    t = b * steps + j
    slot = lax.rem(t, 2)

    def page_copies(bb, jj, sl, i):
        pg = pt_ref[bb, jj * npages + i]
        return (pltpu.make_async_copy(ckv_hbm.at[layer, pg], cbuf.at[sl, pl.ds(i * page, page), :], csem.at[sl]),
                pltpu.make_async_copy(kpt_hbm.at[layer, pg], kbuf.at[sl, i], ksem.at[sl]))

    @pl.when(t == 0)
    def _():
        lhs_scr[0:hn, :] = wukt_ref[...]
        for i in range(npages):
            for cp in page_copies(0, 0, 0, i):
                cp.start()

    @pl.when(j == 0)
    def _():
        lhs_scr[hn:hn + hq, :] = qlat_ref[...]
        m_scr[...] = jnp.full(m_scr.shape, -jnp.inf, F32)
        l_scr[...] = jnp.zeros(l_scr.shape, F32)
        acc_scr[...] = jnp.zeros(acc_scr.shape, F32)

    for i in range(npages):
        for cp in page_copies(b, j, slot, i):
            cp.wait()

    more = j + 1 < steps
    nxt_j = jnp.where(more, j + 1, 0)
    nxt_b = jnp.where(more, b, jnp.where(b + 1 < nb, b + 1, 0))
    qpa, qpb = qpa_ref[...], qpb_ref[...]

    def softmax_update(s, vals):
        m_prev = m_scr[...]
        m_new = jnp.maximum(m_prev, jnp.max(s, axis=-1, keepdims=True))
        alpha = jnp.exp(m_prev - m_new)
        p = jnp.exp(s - m_new)
        l_scr[...] = alpha * l_scr[...] + jnp.sum(p, axis=-1, keepdims=True)
        acc_scr[...] = alpha * acc_scr[...] + jnp.dot(p.astype(BF16), vals, preferred_element_type=F32)
        m_scr[...] = m_new

    for pp in range(npages // 2):
        rows = slice(pp * 2 * page, (pp + 1) * 2 * page)
        cb = cbuf[slot, rows, :].astype(BF16)
        cbf_scr[rows, :] = cb
        kpt = jnp.concatenate([kbuf[slot, 2 * pp], kbuf[slot, 2 * pp + 1]], axis=1)
        sc = _score_block(c, nq, lhs_scr, cb, kpt, ctab_ref[:, rows], stab_ref[:, rows], qpa, qpb)
        for h in range(c.mla_heads):
            s_scr[h * nq:(h + 1) * nq, rows] = sc[h]
        for i in (2 * pp, 2 * pp + 1):
            for cp in page_copies(nxt_b, nxt_j, 1 - slot, i):
                cp.start()
        if (pp + 1) % c.softmax_pairs == 0:
            grp = slice((pp + 1 - c.softmax_pairs) * 2 * page, (pp + 1) * 2 * page)
            softmax_update(s_scr[:, grp], cbf_scr[grp, :])

    @pl.when(t == nb * steps - 1)
    def _():
        for i in range(npages):
            for cp in page_copies(nxt_b, nxt_j, 1 - slot, i):
                cp.wait()

    @pl.when(j == steps - 1)
    def _():
        cb = jnp.concatenate([cnew_ref[...], jnp.zeros((page - nq, c.kv_lora), F32)], axis=0).astype(BF16)
        sc = _score_block(c, nq, lhs_scr, cb, kpnew_ref[...], ctabn_ref[...], stabn_ref[...], qpa, qpb)
        s = jnp.concatenate(sc, axis=0)
        col = lax.broadcasted_iota(jnp.int32, s.shape, 1)
        qidx = lax.broadcasted_iota(jnp.int32, s.shape, 0) % nq
        softmax_update(jnp.where(col <= qidx, s, -jnp.inf), cb)
        o_ref[...] = acc_scr[...] / l_scr[...]


def paged_sample_attend(c, layer, page_table, w_ukt, qlat, qpa, qpb, c_new, kpt_new, ctab_t, stab_t,
                        cache_ckv, cache_kpe_t):
    nb, n_pages = page_table.shape
    nq = c_new.shape[1]
    hq = c.mla_heads * nq
    hn = c.mla_heads * c.nope_hd
    npg = min(c.pages_per_step, n_pages)
    assert n_pages % npg == 0 and npg % (2 * c.softmax_pairs) == 0
    steps = n_pages // npg
    rows = npg * c.page_size
    past = n_pages * c.page_size
    const2 = lambda b, j, pt: (0, 0)
    per_b = lambda b, j, pt: (b, 0, 0)
    tab_map = lambda b, j, pt: (0, j)
    in_specs = [pl.BlockSpec((hn, c.kv_lora), const2),
                pl.BlockSpec((None, hq, c.kv_lora), per_b),
                pl.BlockSpec((None, hq, c.rope_hd), per_b), pl.BlockSpec((None, hq, c.rope_hd), per_b),
                pl.BlockSpec((None, nq, c.kv_lora), per_b), pl.BlockSpec((None, c.rope_hd, c.page_size), per_b),
                pl.BlockSpec((c.rope_hd, rows), tab_map), pl.BlockSpec((c.rope_hd, rows), tab_map),
                pl.BlockSpec((c.rope_hd, c.page_size), const2), pl.BlockSpec((c.rope_hd, c.page_size), const2),
                pl.BlockSpec(memory_space=pl.ANY), pl.BlockSpec(memory_space=pl.ANY)]
    grid_spec = pltpu.PrefetchScalarGridSpec(
        num_scalar_prefetch=1, grid=(nb, steps), in_specs=in_specs,
        out_specs=pl.BlockSpec((None, hq, c.kv_lora), per_b),
        scratch_shapes=[pltpu.VMEM((hn + hq, c.kv_lora), BF16),
                        pltpu.VMEM((2, rows, c.kv_lora), F32), pltpu.VMEM((2, npg, c.rope_hd, c.page_size), F32),
                        pltpu.SemaphoreType.DMA((2,)), pltpu.SemaphoreType.DMA((2,)),
                        pltpu.VMEM((rows, c.kv_lora), BF16),
                        pltpu.VMEM((hq, rows), F32), pltpu.VMEM((hq, 1), F32), pltpu.VMEM((hq, 1), F32),
                        pltpu.VMEM((hq, c.kv_lora), F32)])
    return pl.pallas_call(
        functools.partial(_paged_kernel, c=c, layer=layer, npages=npg, nq=nq), grid_spec=grid_spec,
        out_shape=jax.ShapeDtypeStruct((nb, hq, c.kv_lora), F32),
        compiler_params=_params(("arbitrary", "arbitrary")), name="paged_sample_attend",
    )(page_table, w_ukt, qlat, qpa, qpb, c_new, kpt_new, ctab_t[:, :past], stab_t[:, :past],
      ctab_t[:, past:], stab_t[:, past:], cache_ckv, cache_kpe_t)


def _qlat_kernel(q_ref, w_ref, g_ref, o_ref, *, scale):
    q = (q_ref[...].astype(F32) * g_ref[...] * scale).astype(BF16)
    r = lax.dot_general(q, w_ref[...].astype(BF16), NT_DIMS, preferred_element_type=F32)
    o_ref[...] = r.reshape(o_ref.shape).astype(o_ref.dtype)


def absorbed_queries(c, q_full, row_blk, w_uk, layer, g_n, nb, nq):
    ts = nb * nq
    nope_blocks = c.hp // c.nope_hd
    return pl.pallas_call(
        functools.partial(_qlat_kernel, scale=c.qk_hd ** -0.5), grid=(c.mla_heads,),
        in_specs=[pl.BlockSpec((ts, c.nope_hd), lambda h: (row_blk, h * nope_blocks)),
                  pl.BlockSpec((None, c.kv_lora, c.nope_hd), lambda h: (layer, 0, h)),
                  pl.BlockSpec((1, c.nope_hd), lambda h: (0, 0))],
        out_specs=pl.BlockSpec((nb, None, nq, c.kv_lora), lambda h: (0, h, 0, 0)),
        out_shape=jax.ShapeDtypeStruct((nb, c.mla_heads, nq, c.kv_lora), BF16),
        compiler_params=_params(("parallel",)), name="absorbed_queries",
    )(q_full, w_uk, g_n)


def _latent_out_kernel(ol_ref, w_ref, g_ref, *rest):
    o_ref = rest[-1]
    ol = ol_ref[...]
    ol = ol.reshape(ol.shape[0] * ol.shape[1], ol.shape[2]).astype(BF16)
    o = jnp.dot(ol, w_ref[...].astype(BF16), preferred_element_type=F32)
    o_ref[...] = (o * _silu(g_ref[...])).astype(o_ref.dtype)


def latent_out(c, o_lat, w_uv, layer, z, row_blk, prev):
    nb, heads, nq, _ = o_lat.shape
    ts = nb * nq
    ga_blk = _zblk(c, "g_a", c.v_hd)
    in_specs = [pl.BlockSpec((nb, None, nq, c.kv_lora), lambda h: (0, h, 0, 0)),
                pl.BlockSpec((None, c.kv_lora, c.v_hd), lambda h: (layer, 0, h)),
                pl.BlockSpec((ts, c.v_hd), lambda h: (row_blk, ga_blk + h))]
    args = [o_lat, w_uv, z]
    aliases = _alias_prev(in_specs, args, prev, 0)
    return pl.pallas_call(
        _latent_out_kernel, grid=(heads,), in_specs=in_specs,
        out_specs=pl.BlockSpec((ts, c.v_hd), lambda h: (row_blk, h)),
        out_shape=jax.ShapeDtypeStruct(prev.shape, prev.dtype), input_output_aliases=aliases,
        compiler_params=_params(("parallel",)), name="latent_out",
    )(*args)


def _gla_kernel(q_ref, k_ref, v_ref, slab_ref, wg_ref, bg_ref, gb_ref, og_ref, *rest, c, chunk, has_s0,
                n_alias):
    rest = list(rest)
    s0_ref = rest.pop(0) if has_s0 else None
    o_ref, sout_ref, s_scr = rest[n_alias:]
    ci = pl.program_id(1)

    @pl.when(ci == 0)
    def _():
        if has_s0:
            s_scr[...] = s0_ref[...]
        else:
            s_scr[...] = jnp.zeros(s_scr.shape, F32)

    khd, vhd = c.gla_khd, c.gla_vhd
    x = jnp.dot(slab_ref[...].astype(BF16), wg_ref[...], preferred_element_type=F32) + bg_ref[...]
    logf = (jnp.minimum(x, 0.0) - jnp.log(1.0 + jnp.exp(-jnp.abs(x)))) / c.gate_tau
    row = lax.broadcasted_iota(jnp.int32, (chunk, chunk), 0)
    col = lax.broadcasted_iota(jnp.int32, (chunk, chunk), 1)
    tril = (col <= row).astype(F32)
    cum_all = jnp.dot(tril, logf, preferred_element_type=F32, precision=lax.Precision.HIGHEST)
    for h in range(c.gla_heads):
        ks, vs = slice(h * khd, (h + 1) * khd), slice(h * vhd, (h + 1) * vhd)
        cum = cum_all[:, ks]
        last = cum[chunk - 1:chunk, :]
        q = q_ref[:, ks] * khd ** -0.5
        k = k_ref[:, ks]
        v = v_ref[:, vs]
        s_prev = s_scr[h]
        inter = jnp.dot(q * jnp.exp(cum), s_prev, preferred_element_type=F32)
        a = jnp.zeros((chunk, chunk), F32)
        for s in range(chunk):
            t0 = (s // SUBLANES) * SUBLANES
            e = jnp.exp(cum[t0:, :] - cum[s:s + 1, :])
            colv = jnp.sum(q[t0:, :] * (k[s:s + 1, :] * e), axis=-1, keepdims=True)
            if t0:
                colv = jnp.concatenate([jnp.zeros((t0, 1), F32), colv], axis=0)
            a = jnp.where(col == s, colv, a)
        a = jnp.where(col <= row, a, 0.0)
        o = inter + jnp.dot(a, v, preferred_element_type=F32)
        ms = jnp.mean(o * o, axis=-1, keepdims=True)
        on = o * lax.rsqrt(ms + EPS) * og_ref[...]
        o_ref[:, vs] = (on * _silu(gb_ref[:, vs])).astype(o_ref.dtype)
        kt = k * jnp.exp(last - cum)
        upd = lax.dot_general(kt, v, TN_DIMS, preferred_element_type=F32)
        dcol = jnp.exp(jnp.transpose(jnp.broadcast_to(last, (LANES, khd))))
        for n in range(vhd // LANES):
            sl = slice(n * LANES, (n + 1) * LANES)
            s_scr[h, :, sl] = dcol * s_prev[:, sl] + upd[:, sl]

    @pl.when(ci == pl.num_programs(1) - 1)
    def _():
        sout_ref[...] = s_scr[...]


def gla(c, z, zs, w_gate_pad, b_gate, gla_o_g, s0, layer, depth, *, nb, length, row0, o_prev, s_prev):
    t = z.shape[0]
    chunk = math.gcd(length, c.gla_chunk)
    nch = length // chunk
    assert row0 % chunk == 0
    heads, khd, vhd = c.gla_heads, c.gla_khd, c.gla_vhd
    base = row0 // chunk
    kd, vd = heads * khd, heads * vhd
    rmap = lambda off: (lambda b, ci: (base + b * nch + ci, off))
    const = lambda b, ci: (0, 0)
    in_specs = [pl.BlockSpec((chunk, kd), rmap(_zblk(c, "gq", kd))),
                pl.BlockSpec((chunk, kd), rmap(_zblk(c, "gk", kd))),
                pl.BlockSpec((chunk, vd), rmap(_zblk(c, "gv", vd))),
                pl.BlockSpec((chunk, LANES), rmap(0)),
                pl.BlockSpec((LANES, kd), const), pl.BlockSpec((1, kd), const),
                pl.BlockSpec((chunk, vd), rmap(_zblk(c, "g_b", vd))),
                pl.BlockSpec((1, vhd), const)]
    args = [z, z, z, zs, w_gate_pad, b_gate.reshape(1, -1), z, gla_o_g.reshape(1, -1)]
    smap = lambda b, ci: (layer, b, 0, 0, 0)
    s_blk = (None, None, heads, khd, vhd)
    if s0 is not None:
        in_specs.append(pl.BlockSpec(s_blk, smap))
        args.append(s0)
    aliases = {}
    aliases.update(_alias_prev(in_specs, args, o_prev, 0))
    aliases.update(_alias_prev(in_specs, args, s_prev, 1))
    return pl.pallas_call(
        functools.partial(_gla_kernel, c=c, chunk=chunk, has_s0=s0 is not None, n_alias=len(aliases)),
        grid=(nb, nch), in_specs=in_specs,
        out_specs=[pl.BlockSpec((chunk, vd), rmap(0)), pl.BlockSpec(s_blk, smap)],
        out_shape=[jax.ShapeDtypeStruct((t, c.branch_w), BF16),
                   jax.ShapeDtypeStruct((depth, nb, heads, khd, vhd), F32)],
        scratch_shapes=[pltpu.VMEM((heads, khd, vhd), F32)], input_output_aliases=aliases,
        compiler_params=_params(("parallel", "arbitrary")), name="gla",
    )(*args)


def _mem_attn_kernel(q_ref, g_ref, k_ref, v_ref, gc_ref, *rest, heads, scale):
    o_ref = rest[-1]
    hd = q_ref.shape[1] // heads
    for h in range(heads):
        sl = slice(h * hd, (h + 1) * hd)
        q = q_ref[:, sl]
        ms = jnp.mean(q * q, axis=-1, keepdims=True)
        qn = (q * lax.rsqrt(ms + EPS) * g_ref[...]).astype(BF16)
        s = lax.dot_general(qn, k_ref[:, sl].astype(BF16), NT_DIMS, preferred_element_type=F32) * scale
        m = jnp.max(s, axis=-1, keepdims=True)
        p = jnp.exp(s - m)
        p = p / jnp.sum(p, axis=-1, keepdims=True)
        o = jnp.dot(p.astype(BF16), v_ref[:, sl].astype(BF16), preferred_element_type=F32)
        o_ref[:, sl] = (o * _silu(gc_ref[:, sl])).astype(o_ref.dtype)


def mem_attend(c, z, mem_q_g, mem_k, mem_v, layer, *, nb, length, row0, tq, prev):
    t = z.shape[0]
    tq = min(tq, length)
    nqt = length // tq
    assert row0 % tq == 0
    base = row0 // tq
    heads, hd, n_mem = c.mem_heads, c.mem_hd, mem_k.shape[2]
    w = heads * hd
    q_blk, g_blk = _zblk(c, "mq", w), _zblk(c, "g_c", w)
    kv_spec = pl.BlockSpec((None, None, n_mem, w), lambda b, i: (layer, b, 0, 0))
    in_specs = [pl.BlockSpec((tq, w), lambda b, i: (base + b * nqt + i, q_blk)),
                pl.BlockSpec((1, hd), lambda b, i: (0, 0)), kv_spec, kv_spec,
                pl.BlockSpec((tq, w), lambda b, i: (base + b * nqt + i, g_blk))]
    args = [z, mem_q_g.reshape(1, -1), mem_k, mem_v, z]
    aliases = _alias_prev(in_specs, args, prev, 0)
    return pl.pallas_call(
        functools.partial(_mem_attn_kernel, heads=heads, scale=hd ** -0.5), grid=(nb, nqt),
        in_specs=in_specs,
        out_specs=pl.BlockSpec((tq, w), lambda b, i: (base + b * nqt + i, 0)),
        out_shape=jax.ShapeDtypeStruct((t, c.branch_w), BF16), input_output_aliases=aliases,
        compiler_params=_params(("parallel", "arbitrary")), name="mem_attend",
    )(*args)


def _merge_kernel(ba_ref, bb_ref, bc_ref, w_ref, la_ref, lb_ref, lc_ref, bm_ref, o_ref):
    acc = None
    for n, (b_ref, l_ref) in enumerate(((ba_ref, la_ref), (bb_ref, lb_ref), (bc_ref, lc_ref))):
        proj = jnp.dot(b_ref[...], w_ref[n].astype(BF16), preferred_element_type=F32)
        term = _sigmoid(l_ref[...] + bm_ref[n:n + 1, :]) * proj
        acc = term if acc is None else acc + term
    o_ref[...] = acc.astype(o_ref.dtype)


def merge(c, br_a, br_b, br_c, w_branch, layer, z, b_merge):
    t = br_a.shape[0]
    d = c.d_model
    tm, tn = min(c.tm_merge, t), min(c.tn_merge, d)
    nj = d // tn
    l_blk = _zblk(c, "m_logit", tn)
    bmap = lambda j, i: (i, 0)
    lmap = lambda n: (lambda j, i: (i, l_blk + n * nj + j))
    return pl.pallas_call(
        _merge_kernel, grid=(nj, t // tm),
        in_specs=[pl.BlockSpec((tm, c.branch_w), bmap)] * 3
        + [pl.BlockSpec((None, c.n_branch, c.branch_w, tn), lambda j, i: (layer, 0, 0, j))]
        + [pl.BlockSpec((tm, tn), lmap(n)) for n in range(3)]
        + [pl.BlockSpec((c.n_branch, tn), lambda j, i: (0, j))],
        out_specs=pl.BlockSpec((tm, tn), lambda j, i: (i, j)),
        out_shape=jax.ShapeDtypeStruct((t, d), BF16),
        compiler_params=_params(("parallel", "parallel")), name="merge",
    )(br_a, br_b, br_c, w_branch, z, z, z, b_merge)


def _rope_tables(c, pos):
    half = c.rope_hd // 2
    inv = jnp.power(c.rope_theta, -jnp.arange(half, dtype=F32) / half)
    ang = pos.astype(F32)[:, None] * inv[None, :]
    cos, sin = jnp.cos(ang), jnp.sin(ang)
    z = jnp.zeros((pos.shape[0], LANES - c.rope_hd), F32)
    zh = jnp.zeros_like(cos)
    slab_tabs = (jnp.concatenate([cos, cos, z], 1), jnp.concatenate([-sin, zh, z], 1),
                 jnp.concatenate([zh, sin, z], 1))
    return slab_tabs, jnp.concatenate([cos, cos], 1).T, jnp.concatenate([sin, -sin], 1).T


def _in_proj_offsets(c):
    offs = []
    for n in Z_NAMES:
        offs += [c.in_off[n] + k * c.tn_in for k in range(c.in_size[n] // c.tn_in)]
    assert all(o % OFFSET_UNIT == 0 for o in offs)
    return jnp.asarray([o // OFFSET_UNIT for o in offs], jnp.int32)


def _forward(c, x_prompt, x_sample, mem_prompt, cache_ckv, cache_kpe, cache_mem_k, cache_mem_v, state_gla,
             page_table, **w):
    c = _derive(c)
    batch, seq, d = x_prompt.shape
    nb, nq = x_sample.shape[:2]
    n_mem = mem_prompt.shape[1]
    depth = w["w_in"].shape[0]
    tp, ts = batch * seq, nb * nq
    t = tp + ts
    past = page_table.shape[1] * c.page_size
    assert tp % ts == 0
    heads = c.mla_heads

    pos_all = jnp.concatenate([jnp.tile(jnp.arange(seq, dtype=jnp.int32), batch),
                               jnp.tile(past + jnp.arange(nq, dtype=jnp.int32), nb)])
    tabs, _, _ = _rope_tables(c, pos_all)
    _, ctab_t, stab_t = _rope_tables(c, jnp.arange(past + c.page_size, dtype=jnp.int32))

    x_all = jnp.concatenate([x_prompt.reshape(tp, d), x_sample.reshape(ts, d)], axis=0)
    mem_flat = mem_prompt.reshape(batch * n_mem, d)
    w_in_t = jnp.swapaxes(w["w_in"], 1, 2)
    cache_kpe_t = jnp.swapaxes(cache_kpe, 2, 3)
    mem_k_all = cache_mem_k.reshape(depth, nb, n_mem, -1).astype(BF16)
    mem_v_all = cache_mem_v.reshape(depth, nb, n_mem, -1).astype(BF16)
    in_offs = _in_proj_offsets(c)
    half = c.rope_hd // 2
    outs = {k: [] for k in ("ckv_p", "kpe_p", "gla_p", "mk_p", "mv_p", "ckv_s", "kpe_s")}
    gla_s = None

    for l in range(depth):
        wq = w["w_uq"][l].reshape(c.q_lora, heads, c.qk_hd)
        w_uq_pad = jnp.pad(wq, ((0, 0), (0, 0), (0, c.hp - c.qk_hd))).reshape(c.q_lora, heads * c.hp)
        w_uq_pad = w_uq_pad.astype(BF16)
        padg = lambda g: jnp.pad(g, (0, c.hp - c.qk_hd)).reshape(1, c.hp)
        qg_pad, kg_pad = padg(w["mla_q_g"][l]), padg(w["mla_k_g"][l])
        w_ukt = w["w_uk"][l].T.astype(BF16)
        w_gate = jnp.zeros((LANES, c.gla_kd), F32).at[c.rope_hd:c.rope_hd + c.gate_rank].set(w["w_gate2"][l])
        w_gate = w_gate.astype(BF16)

        h = rms_rows(x_all, w["norm_w"][l], c.tm_rows)
        z, zs = in_proj(c, h, w_in_t, l, in_offs, tm=c.tm_mm, tn=c.tn_in)
        m_n = rms_rows(mem_flat, w["mem_norm_w"][l], c.tm_rows)
        mk = matmul(m_n, w["w_mk"], tm=c.tm_mm, tn=c.mem_hd, mode="tile_norm", extra=w["mem_k_g"][l],
                    layer=l, name="mem_k")
        mv = matmul(m_n, w["w_mv"], tm=c.tm_mm, tn=c.tn_in, layer=l, name="mem_v")
        q_full = q_prep(c, z, w["q_lora_g"][l], w_uq_pad, qg_pad, tabs)
        ckv_n, kpe, k_full, v_full = kv_prep(c, z, zs, w["kv_lora_g"][l], w["w_uk"], w["w_uv"], l, kg_pad, tabs)
        br_a = flash_prompt(c, q_full, k_full, v_full, z, batch, seq)
        g_n = w["mla_k_g"][l][:c.nope_hd].reshape(1, -1)
        g_p = w["mla_k_g"][l][c.nope_hd:]
        qlat = absorbed_queries(c, q_full, tp // ts, w["w_uk"], l, g_n, nb, nq)
        qlat = qlat.reshape(nb, heads * nq, c.kv_lora)
        q_pe = q_full[tp:].reshape(nb, nq, heads, c.hp)[..., c.nope_hd:c.qk_hd].astype(F32)
        q_pe = jnp.swapaxes(q_pe, 1, 2).reshape(nb, heads * nq, c.rope_hd) * c.qk_hd ** -0.5
        qpa = (q_pe * g_p).astype(BF16)
        qpb = (jnp.concatenate([q_pe[..., half:], q_pe[..., :half]], -1) * g_p).astype(BF16)
        c_new = ckv_n[tp:].reshape(nb, nq, c.kv_lora)
        kp_new = kpe[tp:].reshape(nb, nq, c.rope_hd)
        kpt_new = jnp.pad(jnp.swapaxes(kp_new, 1, 2), ((0, 0), (0, 0), (0, c.page_size - nq)))
        o_lat = paged_sample_attend(c, l, page_table, w_ukt, qlat, qpa, qpb, c_new, kpt_new, ctab_t, stab_t,
                                    cache_ckv, cache_kpe_t)
        br_a = latent_out(c, o_lat.reshape(nb, heads, nq, c.kv_lora), w["w_uv"], l, z, tp // ts, br_a)
        br_b, gla_p = gla(c, z, zs, w_gate, w["b_gate"][l], w["gla_o_g"][l], None, 0, 1,
                          nb=batch, length=seq, row0=0, o_prev=None, s_prev=None)
        br_b, gla_s = gla(c, z, zs, w_gate, w["b_gate"][l], w["gla_o_g"][l], state_gla, l, depth,
                          nb=nb, length=nq, row0=tp, o_prev=br_b, s_prev=gla_s)
        mem4 = lambda a: a.reshape(1, batch, n_mem, c.branch_w)
        br_c = mem_attend(c, z, w["mem_q_g"][l], mem4(mk), mem4(mv), 0, nb=batch, length=seq, row0=0,
                          tq=c.tq_mem, prev=None)
        br_c = mem_attend(c, z, w["mem_q_g"][l], mem_k_all, mem_v_all, l, nb=nb, length=nq, row0=tp, tq=nq,
                          prev=br_c)
        mix = merge(c, br_a, br_b, br_c, w["w_branch"], l, z, w["b_merge"][l])
        x_all = matmul(mix, w["w_out"], tm=c.tm_mm, tn=c.tn_in, mode="residual", extra=x_all, layer=l,
                       name="out_proj")

        outs["ckv_p"].append(ckv_n[:tp].reshape(batch, seq, -1))
        outs["kpe_p"].append(kpe[:tp].reshape(batch, seq, -1))
        outs["gla_p"].append(gla_p[0])
        outs["mk_p"].append(mk.reshape(batch, n_mem, c.mem_heads, c.mem_hd))
        outs["mv_p"].append(mv.reshape(batch, n_mem, c.mem_heads, c.mem_hd))
        outs["ckv_s"].append(c_new)
        outs["kpe_s"].append(kp_new)

    st = lambda k: jnp.stack(outs[k])
    return (x_all[:tp].reshape(batch, seq, d), x_all[tp:].reshape(nb, nq, d),
            st("ckv_p"), st("kpe_p"), st("gla_p"), st("mk_p"), st("mv_p"), st("ckv_s"), st("kpe_s"), gla_s)


def kernel(x_prompt, x_sample, mem_prompt, cache_ckv, cache_kpe, cache_mem_k, cache_mem_v, state_gla, page_table, norm_w, w_in, q_lora_g, kv_lora_g, w_uq, mla_q_g, mla_k_g, w_uk, w_uv, w_gate2, b_gate, gla_o_g, mem_norm_w, w_mk, w_mv, mem_q_g, mem_k_g, w_branch, b_merge, w_out):
    return _forward(default_config(), x_prompt, x_sample, mem_prompt, cache_ckv, cache_kpe, cache_mem_k,
                    cache_mem_v, state_gla, page_table,
                    norm_w=norm_w, w_in=w_in, q_lora_g=q_lora_g, kv_lora_g=kv_lora_g, w_uq=w_uq,
                    mla_q_g=mla_q_g, mla_k_g=mla_k_g, w_uk=w_uk, w_uv=w_uv, w_gate2=w_gate2, b_gate=b_gate,
                    gla_o_g=gla_o_g, mem_norm_w=mem_norm_w, w_mk=w_mk, w_mv=w_mv, mem_q_g=mem_q_g,
                    mem_k_g=mem_k_g, w_branch=w_branch, b_merge=b_merge, w_out=w_out)
```
